```python
import math
import jax
import jax.numpy as jnp
from jax import lax
import numpy as np

D_MODEL = 1024
BATCH = 16
SEQ = 4096
DEPTH = 4
DEC_BATCH = 8
DEC_SEQ = 2048
PAST_LEN = 128

N_MIXERS = 3
GRID_W = 64
LN_EPS = 1e-5

M_EXPAND = 2
M_DI = M_EXPAND * D_MODEL
M_HEADDIM = 64
M_HEADS = M_DI // M_HEADDIM
M_STATE = 128
M_GROUPS = 8
M_CONV = 5
M_CHUNK = 128
M_CONV_CH = M_DI + 2 * M_GROUPS * M_STATE
M_IN = M_DI + M_CONV_CH + 2 * M_HEADS
M_NORM_EPS = 1e-5

R_HEADSIZE = 64
R_DIM = D_MODEL
R_HEADS = R_DIM // R_HEADSIZE
R_DECAY_LORA = 64
R_AAA_LORA = 64
R_IN = 4 * R_DIM + 2 * R_DECAY_LORA + 2 * R_AAA_LORA
R_GN_EPS = 64e-5

A_HEADDIM = 128
A_HEADS = D_MODEL // A_HEADDIM
A_KV_HEADS = 2
A_GROUP = A_HEADS // A_KV_HEADS
A_DIM = A_HEADS * A_HEADDIM
A_KV_DIM = A_KV_HEADS * A_HEADDIM
A_IN = 2 * A_DIM + 2 * A_KV_DIM
ROPE_AXIS_DIM = A_HEADDIM // 2
ROPE_THETA = 10000.0
QK_NORM_EPS = 1e-6
Q_BLOCK = 128

kernel_name = 'hybrid_bidir_ssd_rwkv7_axial_gqa_trunk'


def _n_layers_of(kind):
    return len(range(kind, DEPTH, N_MIXERS))


def _layernorm(x, g, b):
    xf = x.astype(jnp.float32)
    mu = jnp.mean(xf, -1, keepdims=True)
    var = jnp.mean(jnp.square(xf - mu), -1, keepdims=True)
    return ((xf - mu) * lax.rsqrt(var + LN_EPS) * g + b).astype(x.dtype)


def _rmsnorm(x, g, eps):
    xf = x.astype(jnp.float32)
    return (xf * lax.rsqrt(jnp.mean(xf * xf, -1, keepdims=True) + eps) * g).astype(x.dtype)


def _centred_dwconv(u, w, b):
    k, c = w.shape
    pad = k // 2
    out = lax.conv_general_dilated(u, w[:, None, :].astype(u.dtype), window_strides=(1,),
                                   padding=[(pad, pad)], dimension_numbers=('NWC', 'WIO', 'NWC'),
                                   feature_group_count=c)
    return out + b


def _ssd_chunked(xh, dt, a, bm, cm):
    bsz, seqlen, nh, hp = xh.shape
    ng, ns = bm.shape[2], bm.shape[3]
    nr = nh // ng
    nc = seqlen // M_CHUNK
    q = M_CHUNK
    x = xh.reshape(bsz, nc, q, ng, nr, hp)
    dtc = dt.astype(jnp.float32).reshape(bsz, nc, q, ng, nr)
    bc = bm.reshape(bsz, nc, q, ng, ns)
    cc = cm.reshape(bsz, nc, q, ng, ns)
    acs = jnp.cumsum(dtc * a.reshape(ng, nr), axis=2)
    xdt = x * dtc[..., None]
    acs_t = jnp.transpose(acs, (0, 1, 3, 4, 2))
    seg = acs_t[..., :, None] - acs_t[..., None, :]
    mask = jnp.tril(jnp.ones((q, q), dtype=bool))
    decay = jnp.exp(jnp.where(mask, seg, -jnp.inf))
    cb = jnp.einsum('bcqgn,bcsgn->bcgqs', cc, bc)
    scores = cb[:, :, :, None] * decay
    y_diag = jnp.einsum('bcgrqs,bcsgrp->bcqgrp', scores, xdt)
    xw = xdt * jnp.exp(acs[:, :, -1:] - acs)[..., None]
    states = jnp.einsum('bcsgn,bcsgrp->bcgrpn', bc, xw)
    chunk_decay = jnp.exp(acs[:, :, -1])

    def step(h, inp):
        st, dec = inp
        return h * dec[..., None, None] + st, h

    h0 = jnp.zeros_like(states[:, 0])
    _, prev = lax.scan(step, h0, (jnp.moveaxis(states, 1, 0), jnp.moveaxis(chunk_decay, 1, 0)))
    prev = jnp.moveaxis(prev, 0, 1)
    y_off = jnp.einsum('bcqgn,bcgrpn->bcqgrp', cc, prev) * jnp.exp(acs)[..., None]
    return (y_diag + y_off).reshape(bsz, seqlen, nh, hp).astype(xh.dtype)


def _mamba2_mixer(x, w_in, conv_w, conv_b, a_log, dt_bias, d_skip, norm_w, w_out):
    bsz, seqlen, _ = x.shape
    proj = x @ w_in
    z = proj[..., :M_DI]
    xbc = proj[..., M_DI:M_DI + M_CONV_CH]
    dt_raw = proj[..., M_DI + M_CONV_CH:]
    xbc = jax.nn.silu(_centred_dwconv(xbc, conv_w, conv_b))
    xh = xbc[..., :M_DI].reshape(bsz, seqlen, M_HEADS, M_HEADDIM)
    bm = xbc[..., M_DI:M_DI + M_GROUPS * M_STATE].reshape(bsz, seqlen, M_GROUPS, M_STATE)
    cm = xbc[..., M_DI + M_GROUPS * M_STATE:].reshape(bsz, seqlen, M_GROUPS, M_STATE)
    dt = jax.nn.softplus(dt_raw.astype(jnp.float32).reshape(bsz, seqlen, 2, M_HEADS) + dt_bias)
    a = -jnp.exp(a_log.astype(jnp.float32))
    y_f = _ssd_chunked(xh, dt[:, :, 0], a[0], bm, cm)
    y_b = jnp.flip(_ssd_chunked(jnp.flip(xh, 1), jnp.flip(dt[:, :, 1], 1), a[1],
                                jnp.flip(bm, 1), jnp.flip(cm, 1)), 1)
    y = y_f + y_b + xh * d_skip[:, None]
    y = y.reshape(bsz, seqlen, M_DI) * jax.nn.silu(z)
    y = _rmsnorm(y.reshape(bsz, seqlen, M_GROUPS, M_DI // M_GROUPS), 1.0, M_NORM_EPS)
    y = y.reshape(bsz, seqlen, M_DI) * norm_w
    return (y @ w_out).astype(x.dtype)


def _token_shift_centred(u):
    prev = jnp.pad(u[:, :-1], ((0, 0), (1, 0), (0, 0)))
    nxt = jnp.pad(u[:, 1:], ((0, 0), (0, 1), (0, 0)))
    return 0.5 * (prev + nxt)


def _rwkv7_scan(r, w, k, v, kk, a):
    bsz, seqlen, nh, n = r.shape

    def step(s, inp):
        r_t, w_t, k_t, v_t, kk_t, a_t = inp
        sa = jnp.einsum('bhvk,bhk->bhv', s, -kk_t)
        s = (s * w_t[:, :, None, :] + sa[..., None] * (kk_t * a_t)[:, :, None, :]
             + v_t[..., None] * k_t[:, :, None, :])
        return s, jnp.einsum('bhvk,bhk->bhv', s, r_t)

    s0 = jnp.zeros((bsz, nh, n, n), jnp.float32)
    seqs = tuple(jnp.moveaxis(t.astype(jnp.float32), 1, 0) for t in (r, w, k, v, kk, a))
    _, y = lax.scan(step, s0, seqs)
    return jnp.moveaxis(y, 0, 1)


def _rwkv7_mixer(x, w_in, mu, w0, w_up, a0, a_up, k_k, k_a, r_k, ln_w, ln_b, w_out):
    bsz, seqlen, _ = x.shape
    hs = (bsz, seqlen, R_HEADS, R_HEADSIZE)
    u = x @ w_in
    u = u + (_token_shift_centred(u) - u) * mu
    r = u[..., :R_DIM]
    k = u[..., R_DIM:2 * R_DIM]
    v = u[..., 2 * R_DIM:3 * R_DIM]
    g = jax.nn.silu(u[..., 3 * R_DIM:4 * R_DIM])
    lw = u[..., 4 * R_DIM:4 * R_DIM + 2 * R_DECAY_LORA].astype(jnp.float32).reshape(bsz, seqlen, 2, R_DECAY_LORA)
    la = u[..., 4 * R_DIM + 2 * R_DECAY_LORA:].astype(jnp.float32).reshape(bsz, seqlen, 2, R_AAA_LORA)
    w_log = -jax.nn.softplus(-(w0 + jnp.einsum('blzr,zrc->blzc', jnp.tanh(lw), w_up))) - 0.5
    decay = jnp.exp(-jnp.exp(w_log.astype(jnp.float32)))
    a = jax.nn.sigmoid((a0 + jnp.einsum('blzr,zrc->blzc', la, a_up)).astype(jnp.float32))
    kk = (k * k_k).astype(jnp.float32).reshape(hs)
    kk = kk / jnp.maximum(jnp.sqrt(jnp.sum(kk * kk, -1, keepdims=True)), 1e-12)
    k_dir = (k[:, :, None] * (1.0 + (a - 1.0) * k_a)).reshape(bsz, seqlen, 2, R_HEADS, R_HEADSIZE)
    decay = decay.reshape(bsz, seqlen, 2, R_HEADS, R_HEADSIZE)
    a_h = a.reshape(bsz, seqlen, 2, R_HEADS, R_HEADSIZE)
    r_h = r.reshape(hs)
    v_h = v.reshape(hs)
    y_f = _rwkv7_scan(r_h, decay[:, :, 0], k_dir[:, :, 0], v_h, kk, a_h[:, :, 0])
    fl = lambda t: jnp.flip(t, 1)
    y_b = fl(_rwkv7_scan(fl(r_h), fl(decay[:, :, 1]), fl(k_dir[:, :, 1]), fl(v_h), fl(kk), fl(a_h[:, :, 1])))
    y = y_f + y_b
    mu_y = jnp.mean(y, -1, keepdims=True)
    var_y = jnp.mean(jnp.square(y - mu_y), -1, keepdims=True)
    y = ((y - mu_y) * lax.rsqrt(var_y + R_GN_EPS)).reshape(bsz, seqlen, R_DIM) * ln_w + ln_b
    bonus = jnp.sum(jnp.sum(r_h[:, :, None] * k_dir * r_k, -1, keepdims=True), axis=2) * v_h
    out = ((y + bonus.reshape(bsz, seqlen, R_DIM)) * g).astype(x.dtype)
    return out @ w_out


def _axial_rope_tables(seqlen):
    rows = seqlen // GRID_W
    row = jnp.repeat(jnp.arange(rows, dtype=jnp.float32), GRID_W)
    col = jnp.tile(jnp.arange(GRID_W, dtype=jnp.float32), rows)
    inv = ROPE_THETA ** (-jnp.arange(0, ROPE_AXIS_DIM, 2, dtype=jnp.float32) / ROPE_AXIS_DIM)
    ang_r = row[:, None] * inv[None]
    ang_c = col[:, None] * inv[None]
    return jnp.cos(ang_r), jnp.sin(ang_r), jnp.cos(ang_c), jnp.sin(ang_c)


def _rotate(x, cos, sin):
    half = x.shape[-1] // 2
    x1, x2 = x[..., :half], x[..., half:]
    c, s = cos[:, None, :], sin[:, None, :]
    return jnp.concatenate([x1 * c - x2 * s, x2 * c + x1 * s], -1)


def _apply_axial_rope(x, tables):
    cr, sr, cc, sc = tables
    out = jnp.concatenate([_rotate(x[..., :ROPE_AXIS_DIM], cr, sr),
                           _rotate(x[..., ROPE_AXIS_DIM:], cc, sc)], -1)
    return out.astype(x.dtype)


def _axial_gqa_mixer(x, w_in, q_norm, k_norm, w_out):
    bsz, seqlen, _ = x.shape
    u = x @ w_in
    q = u[..., :A_DIM].reshape(bsz, seqlen, A_HEADS, A_HEADDIM)
    k = u[..., A_DIM:A_DIM + A_KV_DIM].reshape(bsz, seqlen, A_KV_HEADS, A_HEADDIM)
    v = u[..., A_DIM + A_KV_DIM:A_DIM + 2 * A_KV_DIM].reshape(bsz, seqlen, A_KV_HEADS, A_HEADDIM)
    g = jax.nn.silu(u[..., A_DIM + 2 * A_KV_DIM:])
    tables = _axial_rope_tables(seqlen)
    q = _apply_axial_rope(_rmsnorm(q, q_norm, QK_NORM_EPS), tables)
    k = _apply_axial_rope(_rmsnorm(k, k_norm, QK_NORM_EPS), tables)
    nblk = seqlen // Q_BLOCK
    qb = q.reshape(bsz, nblk, Q_BLOCK, A_KV_HEADS, A_GROUP, A_HEADDIM)
    qb = jnp.moveaxis(qb, 1, 0)
    scale = A_HEADDIM ** -0.5

    def block(qblk):
        s = jnp.einsum('bqkgd,bskd->bkgqs', qblk, k).astype(jnp.float32) * scale
        p = jax.nn.softmax(s, axis=-1)
        return jnp.einsum('bkgqs,bskd->bqkgd', p.astype(v.dtype), v)

    o = lax.map(block, qb)
    o = jnp.moveaxis(o, 0, 1).reshape(bsz, seqlen, A_DIM)
    return (o * g) @ w_out


def _trunk(x, ln_g, ln_b, m_w_in, m_conv_w, m_conv_b, m_a_log, m_dt_bias, m_d, m_norm_w, m_w_out,
           r_w_in, r_mu, r_w0, r_w_up, r_a0, r_a_up, r_k_k, r_k_a, r_r_k, r_ln_w, r_ln_b, r_w_out,
           a_w_in, a_q_norm, a_k_norm, a_w_out):
    alpha = (2.0 * DEPTH) ** 0.25
    for i in range(DEPTH):
        j = i // N_MIXERS
        kind = i % N_MIXERS
        if kind == 0:
            h = _mamba2_mixer(x, m_w_in[j], m_conv_w[j], m_conv_b[j], m_a_log[j], m_dt_bias[j],
                              m_d[j], m_norm_w[j], m_w_out[j])
        elif kind == 1:
            h = _rwkv7_mixer(x, r_w_in[j], r_mu[j], r_w0[j], r_w_up[j], r_a0[j], r_a_up[j],
                             r_k_k[j], r_k_a[j], r_r_k[j], r_ln_w[j], r_ln_b[j], r_w_out[j])
        else:
            h = _axial_gqa_mixer(x, a_w_in[j], a_q_norm[j], a_k_norm[j], a_w_out[j])
        x = _layernorm(alpha * x + h, ln_g[i], ln_b[i])
    return x


def setup_inputs(seed: int = 0) -> dict:
    key = jax.random.key(seed)
    ks = jax.random.split(key, 32)
    f32 = jnp.float32
    na, nb, nc = _n_layers_of(0), _n_layers_of(1), _n_layers_of(2)
    beta = (8.0 * DEPTH) ** -0.25
    nrm = lambda k, s, sc: jax.random.normal(k, s, f32) * sc
    dt0 = jnp.exp(jax.random.uniform(ks[7], (na, 2, M_HEADS), f32)
                  * (math.log(0.1) - math.log(0.001)) + math.log(0.001))
    return {
        'x_prompt': nrm(ks[0], (BATCH, SEQ, D_MODEL), 1.0),
        'x_sample': nrm(ks[1], (DEC_BATCH, DEC_SEQ, D_MODEL), 1.0),
        'ln_g': 1.0 + nrm(ks[2], (DEPTH, D_MODEL), 0.02),
        'ln_b': nrm(ks[3], (DEPTH, D_MODEL), 0.02),
        'm_w_in': nrm(ks[4], (na, D_MODEL, M_IN), D_MODEL ** -0.5),
        'm_conv_w': nrm(ks[5], (na, M_CONV, M_CONV_CH), M_CONV ** -0.5),
        'm_conv_b': nrm(ks[6], (na, M_CONV_CH), 0.02),
        'm_a_log': jnp.log(jax.random.uniform(ks[8], (na, 2, M_HEADS), f32, 1.0, 16.0)),
        'm_dt_bias': dt0 + jnp.log(-jnp.expm1(-dt0)),
        'm_d': 1.0 + nrm(ks[9], (na, M_HEADS), 0.1),
        'm_norm_w': 1.0 + nrm(ks[10], (na, M_DI), 0.02),
        'm_w_out': nrm(ks[11], (na, M_DI, D_MODEL), beta * M_DI ** -0.5),
        'r_w_in': nrm(ks[12], (nb, D_MODEL, R_IN), D_MODEL ** -0.5),
        'r_mu': jax.random.uniform(ks[13], (nb, R_IN), f32),
        'r_w0': jax.random.uniform(ks[14], (nb, 2, R_DIM), f32, -6.0, -1.0),
        'r_w_up': nrm(ks[15], (nb, 2, R_DECAY_LORA, R_DIM), 0.5 * R_DECAY_LORA ** -0.5),
        'r_a0': nrm(ks[16], (nb, 2, R_DIM), 0.1),
        'r_a_up': nrm(ks[17], (nb, 2, R_AAA_LORA, R_DIM), 0.5 * R_AAA_LORA ** -0.5),
        'r_k_k': 0.85 + nrm(ks[18], (nb, R_DIM), 0.05),
        'r_k_a': 1.0 + nrm(ks[19], (nb, R_DIM), 0.05),
        'r_r_k': nrm(ks[20], (nb, R_HEADS, R_HEADSIZE), 0.1),
        'r_ln_w': 1.0 + nrm(ks[21], (nb, R_DIM), 0.02),
        'r_ln_b': nrm(ks[22], (nb, R_DIM), 0.02),
        'r_w_out': nrm(ks[23], (nb, R_DIM, D_MODEL), beta * R_DIM ** -0.5),
        'a_w_in': nrm(ks[24], (nc, D_MODEL, A_IN), D_MODEL ** -0.5),
        'a_q_norm': 1.0 + nrm(ks[25], (nc, A_HEADDIM), 0.02),
        'a_k_norm': 1.0 + nrm(ks[26], (nc, A_HEADDIM), 0.02),
        'a_w_out': nrm(ks[27], (nc, A_DIM, D_MODEL), beta * A_DIM ** -0.5),
    }


def reference(x_prompt, x_sample, ln_g, ln_b, m_w_in, m_conv_w, m_conv_b, m_a_log, m_dt_bias, m_d,
              m_norm_w, m_w_out, r_w_in, r_mu, r_w0, r_w_up, r_a0, r_a_up, r_k_k, r_k_a, r_r_k,
              r_ln_w, r_ln_b, r_w_out, a_w_in, a_q_norm, a_k_norm, a_w_out):
    y_prompt = _trunk(x_prompt, ln_g, ln_b, m_w_in, m_conv_w, m_conv_b, m_a_log, m_dt_bias, m_d,
                      m_norm_w, m_w_out, r_w_in, r_mu, r_w0, r_w_up, r_a0, r_a_up, r_k_k, r_k_a,
                      r_r_k, r_ln_w, r_ln_b, r_w_out, a_w_in, a_q_norm, a_k_norm, a_w_out)
    y_sample = _trunk(x_sample, ln_g, ln_b, m_w_in, m_conv_w, m_conv_b, m_a_log, m_dt_bias, m_d,
                      m_norm_w, m_w_out, r_w_in, r_mu, r_w0, r_w_up, r_a0, r_a_up, r_k_k, r_k_a,
                      r_r_k, r_ln_w, r_ln_b, r_w_out, a_w_in, a_q_norm, a_k_norm, a_w_out)
    return (y_prompt, y_sample)
```

```python
import functools
import math

import jax
import jax.numpy as jnp
from jax import lax
from jax.experimental import pallas as pl
from jax.experimental.pallas import tpu as pltpu

F32 = jnp.float32
BF16 = jnp.bfloat16

D_MODEL = 1024
DEPTH = 4
N_MIXERS = 3
GRID_W = 64
LN_EPS = 1e-5

M_DI = 2048
M_HEADDIM = 64
M_HEADS = 32
M_STATE = 128
M_GROUPS = 8
M_HPG = M_HEADS // M_GROUPS
M_GW = M_HPG * M_HEADDIM
M_CONV = 5
M_CHUNK = 128
M_CONV_CH = M_DI + 2 * M_GROUPS * M_STATE
M_NORM_EPS = 1e-5
M_DT_PAD = 128

R_HEADSIZE = 64
R_DIM = 1024
R_HEADS = 16
R_LORA = 64
R_IN = 4 * R_DIM + 4 * R_LORA
R_GN_EPS = 64e-5
R_CHUNK = 64

A_HEADDIM = 128
A_HEADS = 8
A_KV_HEADS = 2
A_GROUP = A_HEADS // A_KV_HEADS
A_DIM = A_HEADS * A_HEADDIM
A_KV_DIM = A_KV_HEADS * A_HEADDIM
ROPE_AXIS_DIM = A_HEADDIM // 2
ROPE_THETA = 10000.0
QK_NORM_EPS = 1e-6

SUBLANES = 8
VMEM_LIMIT = 56 * 1024 * 1024


def _cparams(sem):
    return pltpu.CompilerParams(dimension_semantics=sem, vmem_limit_bytes=VMEM_LIMIT)


def _dot(a, b):
    return jnp.dot(a, b, preferred_element_type=F32)


def _dot_nt(a, b):
    return lax.dot_general(a, b, (((1,), (1,)), ((), ())), preferred_element_type=F32)


def _dot_tn(a, b):
    return lax.dot_general(a, b, (((0,), (0,)), ((), ())), preferred_element_type=F32)


def _split3(x):
    hi = x.astype(BF16)
    r1 = x - hi.astype(F32)
    mid = r1.astype(BF16)
    lo = (r1 - mid.astype(F32)).astype(BF16)
    return hi, mid, lo


def _tri_cumsum(tri, x):
    hi, mid, lo = _split3(x)
    return _dot(tri, hi) + _dot(tri, mid) + _dot(tri, lo)


def _dot_f32ish(a, b):
    ah = a.astype(BF16)
    al = (a - ah.astype(F32)).astype(BF16)
    bh = b.astype(BF16)
    bl = (b - bh.astype(F32)).astype(BF16)
    return _dot(ah, bh) + _dot(ah, bl) + _dot(al, bh)


def _silu(x):
    return x * (1.0 / (1.0 + jnp.exp(-x)))


def _softplus(x):
    return jnp.maximum(x, 0.0) + jnp.log(1.0 + jnp.exp(-jnp.abs(x)))


def _proj_kernel(x_ref, w_ref, o_ref):
    o_ref[...] = _dot(x_ref[...].astype(BF16), w_ref[...])


def _proj(x, w, tm=1024, tn=None):
    t, k = x.shape
    n = w.shape[1]
    if tn is None:
        tn = n if n <= 1024 else 1024
    assert t % tm == 0 and n % tn == 0
    return pl.pallas_call(
        _proj_kernel,
        grid=(t // tm, n // tn),
        in_specs=[pl.BlockSpec((tm, k), lambda i, j: (i, 0)),
                  pl.BlockSpec((k, tn), lambda i, j: (0, j))],
        out_specs=pl.BlockSpec((tm, tn), lambda i, j: (i, j)),
        out_shape=jax.ShapeDtypeStruct((t, n), F32),
        compiler_params=_cparams(("parallel", "arbitrary")),
        name="proj",
    )(x, w)


def _out_ln_kernel(alpha, y_ref, w_ref, x_ref, g_ref, b_ref, o_ref):
    h = _dot(y_ref[...].astype(BF16), w_ref[...])
    s = alpha * x_ref[...] + h
    mu = jnp.mean(s, axis=-1, keepdims=True)
    d = s - mu
    var = jnp.mean(d * d, axis=-1, keepdims=True)
    o_ref[...] = d * lax.rsqrt(var + LN_EPS) * g_ref[...] + b_ref[...]


def _out_ln(y, w, x, g, b, alpha, tm=512):
    t, k = y.shape
    d = w.shape[1]
    assert t % tm == 0
    return pl.pallas_call(
        functools.partial(_out_ln_kernel, alpha),
        grid=(t // tm,),
        in_specs=[pl.BlockSpec((tm, k), lambda i: (i, 0)),
                  pl.BlockSpec((k, d), lambda i: (0, 0)),
                  pl.BlockSpec((tm, d), lambda i: (i, 0)),
                  pl.BlockSpec((1, d), lambda i: (0, 0)),
                  pl.BlockSpec((1, d), lambda i: (0, 0))],
        out_specs=pl.BlockSpec((tm, d), lambda i: (i, 0)),
        out_shape=jax.ShapeDtypeStruct((t, d), F32),
        compiler_params=_cparams(("parallel",)),
        name="out_ln",
    )(y, w, x, g.reshape(1, d), b.reshape(1, d))


def _conv_kernel(nblk, tl, prev_ref, cur_ref, next_ref, w_ref, b_ref, o_ref, ext_ref):
    i = pl.program_id(1)
    halo = SUBLANES
    ext_ref[pl.ds(0, halo), :] = jnp.where(i > 0, prev_ref[0], 0.0)
    ext_ref[pl.ds(halo, tl), :] = cur_ref[0]
    ext_ref[pl.ds(halo + tl, halo), :] = jnp.where(i < nblk - 1, next_ref[0], 0.0)
    pad = M_CONV // 2
    acc = b_ref[...] + w_ref[0:1, :] * ext_ref[pl.ds(halo - pad, tl), :]
    for k in range(1, M_CONV):
        acc = acc + w_ref[k:k + 1, :] * ext_ref[pl.ds(halo - pad + k, tl), :]
    o_ref[0] = _silu(acc)


def _conv_silu(xbc, w, b, tl=512, tc=1024):
    bsz, seqlen, c = xbc.shape
    nblk = seqlen // tl
    r = tl // SUBLANES
    nrow = seqlen // SUBLANES
    return pl.pallas_call(
        functools.partial(_conv_kernel, nblk, tl),
        grid=(bsz, nblk, c // tc),
        in_specs=[
            pl.BlockSpec((1, SUBLANES, tc), lambda bb, i, j: (bb, jnp.maximum(i * r - 1, 0), j)),
            pl.BlockSpec((1, tl, tc), lambda bb, i, j: (bb, i, j)),
            pl.BlockSpec((1, SUBLANES, tc), lambda bb, i, j: (bb, jnp.minimum((i + 1) * r, nrow - 1), j)),
            pl.BlockSpec((M_CONV, tc), lambda bb, i, j: (0, j)),
            pl.BlockSpec((1, tc), lambda bb, i, j: (0, j)),
        ],
        out_specs=pl.BlockSpec((1, tl, tc), lambda bb, i, j: (bb, i, j)),
        out_shape=jax.ShapeDtypeStruct((bsz, seqlen, c), F32),
        scratch_shapes=[pltpu.VMEM((tl + 2 * SUBLANES, tc), F32)],
        compiler_params=_cparams(("parallel", "parallel", "parallel")),
        name="conv_silu",
    )(xbc, xbc, xbc, w, b.reshape(1, c))


def _lane_slab(e, lanes):
    q = e.shape[0]
    n = len(lanes)
    lane = lax.broadcasted_iota(jnp.int32, (q, n * M_HEADDIM), 1)
    out = jnp.broadcast_to(e[:, lanes[n - 1]:lanes[n - 1] + 1], (q, n * M_HEADDIM))
    for r in range(n - 2, -1, -1):
        out = jnp.where(lane < (r + 1) * M_HEADDIM, e[:, lanes[r]:lanes[r] + 1], out)
    return out


def _ssd_kernel(rev, final, *refs):
    if final:
        (x_ref, bc_ref, dt_ref, dtb_ref, alog_ref, z_ref, yb_ref, dsk_ref, nw_ref,
         o_ref, h_ref, y_ref) = refs
    else:
        x_ref, bc_ref, dt_ref, dtb_ref, alog_ref, o_ref, h_ref = refs
        y_ref = o_ref.at[0]
    q = M_CHUNK
    c = pl.program_id(1)

    @pl.when(c == 0)
    def _():
        h_ref[...] = jnp.zeros_like(h_ref)

    d = 1 if rev else 0
    x = x_ref[0]
    dt = _softplus(dt_ref[0] + dtb_ref[...])
    dta = dt * (-jnp.exp(alog_ref[...]))
    row = lax.broadcasted_iota(jnp.int32, (q, q), 0)
    col = lax.broadcasted_iota(jnp.int32, (q, q), 1)
    mask = (col >= row) if rev else (col <= row)
    acs = _tri_cumsum(jnp.where(mask, 1.0, 0.0).astype(BF16), dta)
    end = acs[0:1, :] if rev else acs[q - 1:q, :]
    e_in = dt * jnp.exp(end - acs)
    e_out = jnp.exp(acs)
    acs_t = acs.T
    dt_t = dt.T

    for g in range(M_GROUPS):
        lanes = [d * M_HEADS + g * M_HPG + r for r in range(M_HPG)]
        bm = bc_ref[0, :, g * M_STATE:(g + 1) * M_STATE].astype(BF16)
        cm = bc_ref[0, :, (M_GROUPS + g) * M_STATE:(M_GROUPS + g + 1) * M_STATE].astype(BF16)
        xg = x[:, g * M_GW:(g + 1) * M_GW]
        cb = _dot_nt(cm, bm)
        hg = h_ref[g]
        y_off = _dot(cm, hg.astype(BF16)) * _lane_slab(e_out, lanes)
        xw = (xg * _lane_slab(e_in, lanes)).astype(BF16)
        st = _dot_tn(bm, xw)
        h_ref[g] = hg * _lane_slab(jnp.exp(end), lanes) + st
        for r in range(M_HPG):
            j = lanes[r]
            seg = acs[:, j:j + 1] - acs_t[j:j + 1, :]
            dec = jnp.exp(jnp.where(mask, seg, -jnp.inf)) * dt_t[j:j + 1, :]
            sc = (cb * dec).astype(BF16)
            xh = xg[:, r * M_HEADDIM:(r + 1) * M_HEADDIM].astype(BF16)
            lo = g * M_GW + r * M_HEADDIM
            y_ref[:, lo:lo + M_HEADDIM] = _dot(sc, xh) + y_off[:, r * M_HEADDIM:(r + 1) * M_HEADDIM]

    if final:
        y = (y_ref[...] + yb_ref[0] + x * dsk_ref[...]) * _silu(z_ref[0])
        gw = M_DI // M_GROUPS
        for g in range(M_GROUPS):
            yg = y[:, g * gw:(g + 1) * gw]
            ms = jnp.mean(yg * yg, axis=-1, keepdims=True)
            o_ref[0, :, g * gw:(g + 1) * gw] = yg * lax.rsqrt(ms + M_NORM_EPS) * nw_ref[:, g * gw:(g + 1) * gw]


def _ssd_pass(rev, final, xa, dtr, dtb, alog, extra=()):
    bsz, seqlen, _ = xa.shape
    nc = seqlen // M_CHUNK
    ci = (lambda c: nc - 1 - c) if rev else (lambda c: c)
    tok = lambda width, blk: pl.BlockSpec((1, M_CHUNK, width), lambda b, c: (b, ci(c), blk))
    par = lambda width: pl.BlockSpec((1, width), lambda b, c: (0, 0))
    in_specs = [tok(M_DI, 0), tok(M_DI, 1), tok(M_DT_PAD, 0), par(M_DT_PAD), par(M_DT_PAD)]
    scratch = [pltpu.VMEM((M_GROUPS, M_STATE, M_GW), F32)]
    if final:
        in_specs += [tok(M_DI, 0), tok(M_DI, 0), par(M_DI), par(M_DI)]
        scratch.append(pltpu.VMEM((M_CHUNK, M_DI), F32))
    return pl.pallas_call(
        functools.partial(_ssd_kernel, rev, final),
        grid=(bsz, nc),
        in_specs=in_specs,
        out_specs=tok(M_DI, 0),
        out_shape=jax.ShapeDtypeStruct((bsz, seqlen, M_DI), F32),
        scratch_shapes=scratch,
        compiler_params=_cparams(("parallel", "arbitrary")),
        name="ssd_final" if final else "ssd_bwd",
    )(xa, xa, dtr, dtb, alog, *extra)


def _mamba_layer(x, bsz, seqlen, w_in, conv_w, conv_b, a_log, dt_bias, d_skip, norm_w, w_out, ln_g, ln_b, alpha):
    t = bsz * seqlen
    wz = w_in[:, :M_DI].astype(BF16)
    wx = w_in[:, M_DI:M_DI + M_CONV_CH].astype(BF16)
    wdt = jnp.pad(w_in[:, M_DI + M_CONV_CH:], ((0, 0), (0, M_DT_PAD - 2 * M_HEADS))).astype(BF16)
    z = _proj(x, wz)
    xbc = _proj(x, wx)
    dtr = _proj(x, wdt).reshape(bsz, seqlen, M_DT_PAD)
    xa = _conv_silu(xbc.reshape(bsz, seqlen, M_CONV_CH), conv_w, conv_b)
    pad = (0, M_DT_PAD - 2 * M_HEADS)
    dtb = jnp.pad(dt_bias.reshape(-1), pad).reshape(1, M_DT_PAD)
    alog = jnp.pad(a_log.reshape(-1), pad).reshape(1, M_DT_PAD)
    y_b = _ssd_pass(True, False, xa, dtr, dtb, alog)
    dsk = jnp.repeat(d_skip, M_HEADDIM).reshape(1, M_DI)
    yn = _ssd_pass(False, True, xa, dtr, dtb, alog,
                   extra=(z.reshape(bsz, seqlen, M_DI), y_b, dsk, norm_w.reshape(1, M_DI)))
    return _out_ln(yn.reshape(t, M_DI), w_out.astype(BF16), x, ln_g, ln_b, alpha)


def _rwkv_kernel(rev, final, nc, *refs):
    if final:
        (prev_ref, u_ref, next_ref, mu_ref, w0_ref, wup_ref, a0_ref, aup_ref, kk_ref, ka_ref,
         a0b_ref, aupb_ref, rk_ref, lnw_ref, lnb_ref, yb_ref, o_ref, s_ref, ext_ref, y_ref) = refs
    else:
        (prev_ref, u_ref, next_ref, mu_ref, w0_ref, wup_ref, a0_ref, aup_ref, kk_ref, ka_ref,
         o_ref, s_ref, ext_ref) = refs
        y_ref = o_ref.at[0]
    q = R_CHUNK
    n = R_HEADSIZE
    c = pl.program_id(1)
    cc = (nc - 1 - c) if rev else c

    @pl.when(c == 0)
    def _():
        s_ref[...] = jnp.zeros_like(s_ref)

    halo = SUBLANES
    ext_ref[pl.ds(0, halo), :] = jnp.where(cc > 0, prev_ref[0], 0.0)
    ext_ref[pl.ds(halo, q), :] = u_ref[0]
    ext_ref[pl.ds(halo + q, halo), :] = jnp.where(cc < nc - 1, next_ref[0], 0.0)
    u0 = u_ref[0]
    shifted = 0.5 * (ext_ref[pl.ds(halo - 1, q), :] + ext_ref[pl.ds(halo + 1, q), :])
    u = u0 + (shifted - u0) * mu_ref[...]

    d = 1 if rev else 0
    r_all = u[:, 0:R_DIM]
    k_all = u[:, R_DIM:2 * R_DIM]
    v_all = u[:, 2 * R_DIM:3 * R_DIM]
    lw = u[:, 4 * R_DIM + d * R_LORA:4 * R_DIM + (d + 1) * R_LORA]
    la = u[:, 4 * R_DIM + (2 + d) * R_LORA:4 * R_DIM + (3 + d) * R_LORA]
    w_log = -_softplus(-(w0_ref[...] + _dot_f32ish(jnp.tanh(lw), wup_ref[...]))) - 0.5
    ld = -jnp.exp(w_log)
    a = 1.0 / (1.0 + jnp.exp(-(a0_ref[...] + _dot_f32ish(la, aup_ref[...]))))
    kkr = k_all * kk_ref[...]
    k_dir = k_all * (1.0 + (a - 1.0) * ka_ref[...])

    row = lax.broadcasted_iota(jnp.int32, (q, q), 0)
    col = lax.broadcasted_iota(jnp.int32, (q, q), 1)
    incl = (col >= row) if rev else (col <= row)
    strict = (col > row) if rev else (col < row)
    cum = _tri_cumsum(jnp.where(incl, 1.0, 0.0).astype(BF16), ld)
    w_in = jnp.exp(cum)
    w_inv = jnp.exp(-cum)
    w_ex = jnp.exp(cum - ld)
    w_end = w_in[0:1, :] if rev else w_in[q - 1:q, :]
    eye = jnp.where(row == col, 1.0, 0.0)
    row2 = lax.broadcasted_iota(jnp.int32, (q, 2 * q), 0)
    lane2 = lax.broadcasted_iota(jnp.int32, (q, 2 * q), 1)
    col2 = lane2 & (q - 1)
    k_half = lane2 >= q
    incl2 = (col2 >= row2) if rev else (col2 <= row2)
    strict_k2 = ((col2 > row2) if rev else (col2 < row2)) & k_half
    zeros_v = jnp.zeros((q, n), BF16)
    nlev = int(math.log2(q))

    for h in range(R_HEADS):
        sl = slice(h * n, (h + 1) * n)
        kk = kkr[:, sl]
        kk = kk / jnp.maximum(jnp.sqrt(jnp.sum(kk * kk, axis=-1, keepdims=True)), 1e-12)
        rt = r_all[:, sl] * w_in[:, sl]
        kt = k_dir[:, sl] * w_inv[:, sl]
        bt = kk * a[:, sl] * w_inv[:, sl]
        kap = kk * w_ex[:, sl]
        vb = v_all[:, sl].astype(BF16)
        l1 = jnp.concatenate([kap, rt], axis=0).astype(BF16)
        r1 = jnp.concatenate([bt, kt], axis=0).astype(BF16)
        gm = _dot_nt(l1, r1)
        p = jnp.where(strict, -gm[0:q, 0:q], 0.0)
        t = eye + p
        pb = p.astype(BF16)
        p = _dot(pb, pb)
        for lvl in range(1, nlev):
            pb = p.astype(BF16)
            if lvl < nlev - 1:
                both = _dot(jnp.concatenate([t, p], axis=0).astype(BF16), pb)
                t = t + both[0:q]
                p = both[q:2 * q]
            else:
                t = t + _dot(t.astype(BF16), pb)
        s0 = s_ref[h]
        lh = _dot_nt(l1, s0.astype(BF16))
        top = jnp.where(strict_k2, gm[0:q, :], 0.0).astype(BF16)
        xm = lh[0:q] + _dot(top, jnp.concatenate([zeros_v, vb], axis=0))
        um = _dot(t.astype(BF16), xm.astype(BF16)).astype(BF16)
        uv = jnp.concatenate([um, vb], axis=0)
        bot = jnp.where(incl2, jnp.where(k_half, gm[q:2 * q, :], -gm[q:2 * q, :]), 0.0).astype(BF16)
        y_ref[:, sl] = lh[q:2 * q] + _dot(bot, uv)
        we = w_end[:, sl]
        r1e = jnp.concatenate([-(bt * we), kt * we], axis=0).astype(BF16)
        s_ref[h] = s0 * we + _dot_tn(uv, r1e)

    if final:
        a_b = 1.0 / (1.0 + jnp.exp(-(a0b_ref[...] + _dot_f32ish(
            u[:, 4 * R_DIM + 3 * R_LORA:4 * R_DIM + 4 * R_LORA], aupb_ref[...]))))
        k_sum = k_dir + k_all * (1.0 + (a_b - 1.0) * ka_ref[...])
        rkk = r_all * k_sum * rk_ref[...]
        g = _silu(u[:, 3 * R_DIM:4 * R_DIM])
        ytot = y_ref[...] + yb_ref[0]
        for h in range(R_HEADS):
            sl = slice(h * n, (h + 1) * n)
            yh = ytot[:, sl]
            mu_y = jnp.mean(yh, axis=-1, keepdims=True)
            dy = yh - mu_y
            var_y = jnp.mean(dy * dy, axis=-1, keepdims=True)
            bonus = jnp.sum(rkk[:, sl], axis=-1, keepdims=True) * v_all[:, sl]
            o_ref[0, :, sl] = (dy * lax.rsqrt(var_y + R_GN_EPS) * lnw_ref[:, sl] + lnb_ref[:, sl] + bonus) * g[:, sl]


def _rwkv_pass(rev, final, u, params, extra=()):
    bsz, seqlen, width = u.shape
    nc = seqlen // R_CHUNK
    r = R_CHUNK // SUBLANES
    nrow = seqlen // SUBLANES
    ci = (lambda c: nc - 1 - c) if rev else (lambda c: c)
    in_specs = [
        pl.BlockSpec((1, SUBLANES, width), lambda b, c: (b, jnp.maximum(ci(c) * r - 1, 0), 0)),
        pl.BlockSpec((1, R_CHUNK, width), lambda b, c: (b, ci(c), 0)),
        pl.BlockSpec((1, SUBLANES, width), lambda b, c: (b, jnp.minimum((ci(c) + 1) * r, nrow - 1), 0)),
    ]
    for p in params:
        in_specs.append(pl.BlockSpec(p.shape, lambda b, c: (0, 0)))
    scratch = [pltpu.VMEM((R_HEADS, R_HEADSIZE, R_HEADSIZE), F32),
               pltpu.VMEM((R_CHUNK + 2 * SUBLANES, width), F32)]
    if final:
        for p in extra[:-1]:
            in_specs.append(pl.BlockSpec(p.shape, lambda b, c: (0, 0)))
        in_specs.append(pl.BlockSpec((1, R_CHUNK, R_DIM), lambda b, c: (b, ci(c), 0)))
        scratch.append(pltpu.VMEM((R_CHUNK, R_DIM), F32))
    return pl.pallas_call(
        functools.partial(_rwkv_kernel, rev, final, nc),
        grid=(bsz, nc),
        in_specs=in_specs,
        out_specs=pl.BlockSpec((1, R_CHUNK, R_DIM), lambda b, c: (b, ci(c), 0)),
        out_shape=jax.ShapeDtypeStruct((bsz, seqlen, R_DIM), F32),
        scratch_shapes=scratch,
        compiler_params=_cparams(("parallel", "arbitrary")),
        name="rwkv_final" if final else "rwkv_bwd",
    )(u, u, u, *params, *extra)


def _rwkv_layer(x, bsz, seqlen, w_in, mu, w0, w_up, a0, a_up, k_k, k_a, r_k, ln_w, ln_b, w_out, ln_g, ln_bb, alpha):
    t = bsz * seqlen
    u = _proj(x, w_in.astype(BF16), tn=R_IN // 2).reshape(bsz, seqlen, R_IN)
    row = lambda p: p.reshape(1, -1)
    common = lambda d: (row(mu), row(w0[d]), w_up[d], row(a0[d]), a_up[d], row(k_k), row(k_a))
    y_b = _rwkv_pass(True, False, u, common(1))
    out = _rwkv_pass(False, True, u, common(0),
                     extra=(row(a0[1]), a_up[1], row(r_k), row(ln_w), row(ln_b), y_b))
    return _out_ln(out.reshape(t, R_DIM), w_out.astype(BF16), x, ln_g, ln_bb, alpha)


def _rope_tables(seqlen):
    rows = seqlen // GRID_W
    rowp = jnp.repeat(jnp.arange(rows, dtype=F32), GRID_W)
    colp = jnp.tile(jnp.arange(GRID_W, dtype=F32), rows)
    inv = ROPE_THETA ** (-jnp.arange(0, ROPE_AXIS_DIM, 2, dtype=F32) / ROPE_AXIS_DIM)
    ar = rowp[:, None] * inv[None]
    ac = colp[:, None] * inv[None]
    cos = jnp.concatenate([jnp.cos(ar), jnp.cos(ar), jnp.cos(ac), jnp.cos(ac)], axis=-1)
    sin = jnp.concatenate([-jnp.sin(ar), jnp.sin(ar), -jnp.sin(ac), jnp.sin(ac)], axis=-1)
    return cos, sin


def _qk_prep_kernel(scale, q_ref, kv_ref, cos_ref, sin_ref, qn_ref, kn_ref, qo_ref, ko_ref, vo_ref):
    cos = cos_ref[...]
    sin = sin_ref[...]
    lane = lax.broadcasted_iota(jnp.int32, cos.shape, 1)
    half = ROPE_AXIS_DIM // 2
    low = (lane & half) == 0

    def norm_rope(xh, gain, mult):
        ms = jnp.mean(xh * xh, axis=-1, keepdims=True)
        xn = xh * lax.rsqrt(ms + QK_NORM_EPS) * gain
        partner = jnp.where(low, pltpu.roll(xn, A_HEADDIM - half, 1), pltpu.roll(xn, half, 1))
        return (xn * cos + partner * sin) * mult

    for h in range(A_HEADS):
        sl = slice(h * A_HEADDIM, (h + 1) * A_HEADDIM)
        qo_ref[0, :, sl] = norm_rope(q_ref[0, :, sl], qn_ref[...], scale).astype(BF16)
    for h in range(A_KV_HEADS):
        sl = slice(h * A_HEADDIM, (h + 1) * A_HEADDIM)
        ko_ref[0, :, sl] = norm_rope(kv_ref[0, :, sl], kn_ref[...], 1.0).astype(BF16)
    vo_ref[0] = kv_ref[0, :, A_KV_DIM:2 * A_KV_DIM].astype(BF16)


def _attn_kernel(q_ref, k_ref, v_ref, g_ref, o_ref):
    s = _dot_nt(q_ref[0], k_ref[0])
    m = jnp.max(s, axis=-1, keepdims=True)
    p = jnp.exp(s - m)
    l = jnp.sum(p, axis=-1, keepdims=True)
    o = _dot(p.astype(BF16), v_ref[0]) / l
    o_ref[0] = o * _silu(g_ref[0])


def _attn_layer(x, bsz, seqlen, w_in, q_norm, k_norm, w_out, ln_g, ln_b, alpha):
    t = bsz * seqlen
    wq = w_in[:, :A_DIM].astype(BF16)
    wkv = w_in[:, A_DIM:A_DIM + 2 * A_KV_DIM].astype(BF16)
    wg = w_in[:, A_DIM + 2 * A_KV_DIM:].astype(BF16)
    q = _proj(x, wq).reshape(bsz, seqlen, A_DIM)
    kv = _proj(x, wkv).reshape(bsz, seqlen, 2 * A_KV_DIM)
    gate = _proj(x, wg).reshape(bsz, seqlen, A_DIM)
    cos, sin = _rope_tables(seqlen)
    tl = 512
    tok = lambda width: pl.BlockSpec((1, tl, width), lambda b, i: (b, i, 0))
    tab = pl.BlockSpec((tl, A_HEADDIM), lambda b, i: (i, 0))
    par = pl.BlockSpec((1, A_HEADDIM), lambda b, i: (0, 0))
    qr, kr, vb = pl.pallas_call(
        functools.partial(_qk_prep_kernel, A_HEADDIM ** -0.5),
        grid=(bsz, seqlen // tl),
        in_specs=[tok(A_DIM), tok(2 * A_KV_DIM), tab, tab, par, par],
        out_specs=[tok(A_DIM), tok(A_KV_DIM), tok(A_KV_DIM)],
        out_shape=[jax.ShapeDtypeStruct((bsz, seqlen, A_DIM), BF16),
                   jax.ShapeDtypeStruct((bsz, seqlen, A_KV_DIM), BF16),
                   jax.ShapeDtypeStruct((bsz, seqlen, A_KV_DIM), BF16)],
        compiler_params=_cparams(("parallel", "parallel")),
        name="qk_prep",
    )(q, kv, cos, sin, q_norm.reshape(1, A_HEADDIM), k_norm.reshape(1, A_HEADDIM))
    tq = 512
    o = pl.pallas_call(
        _attn_kernel,
        grid=(bsz, A_HEADS, seqlen // tq),
        in_specs=[pl.BlockSpec((1, tq, A_HEADDIM), lambda b, h, i: (b, i, h)),
                  pl.BlockSpec((1, seqlen, A_HEADDIM), lambda b, h, i: (b, 0, h // A_GROUP)),
                  pl.BlockSpec((1, seqlen, A_HEADDIM), lambda b, h, i: (b, 0, h // A_GROUP)),
                  pl.BlockSpec((1, tq, A_HEADDIM), lambda b, h, i: (b, i, h))],
        out_specs=pl.BlockSpec((1, tq, A_HEADDIM), lambda b, h, i: (b, i, h)),
        out_shape=jax.ShapeDtypeStruct((bsz, seqlen, A_DIM), F32),
        compiler_params=_cparams(("parallel", "parallel", "arbitrary")),
        name="attn",
    )(qr, kr, vb, gate)
    return _out_ln(o.reshape(t, A_DIM), w_out.astype(BF16), x, ln_g, ln_b, alpha)


def _trunk(x3, ln_g, ln_b, m_w_in, m_conv_w, m_conv_b, m_a_log, m_dt_bias, m_d, m_norm_w, m_w_out,
           r_w_in, r_mu, r_w0, r_w_up, r_a0, r_a_up, r_k_k, r_k_a, r_r_k, r_ln_w, r_ln_b, r_w_out,
           a_w_in, a_q_norm, a_k_norm, a_w_out):
    bsz, seqlen, _ = x3.shape
    alpha = (2.0 * DEPTH) ** 0.25
    x = x3.reshape(bsz * seqlen, D_MODEL)
    for i in range(DEPTH):
        j = i // N_MIXERS
        kind = i % N_MIXERS
        if kind == 0:
            x = _mamba_layer(x, bsz, seqlen, m_w_in[j], m_conv_w[j], m_conv_b[j], m_a_log[j], m_dt_bias[j],
                             m_d[j], m_norm_w[j], m_w_out[j], ln_g[i], ln_b[i], alpha)
        elif kind == 1:
            x = _rwkv_layer(x, bsz, seqlen, r_w_in[j], r_mu[j], r_w0[j], r_w_up[j], r_a0[j], r_a_up[j],
                            r_k_k[j], r_k_a[j], r_r_k[j].reshape(-1), r_ln_w[j], r_ln_b[j], r_w_out[j],
                            ln_g[i], ln_b[i], alpha)
        else:
            x = _attn_layer(x, bsz, seqlen, a_w_in[j], a_q_norm[j], a_k_norm[j], a_w_out[j],
                            ln_g[i], ln_b[i], alpha)
    return x.reshape(bsz, seqlen, D_MODEL)


def kernel(x_prompt, x_sample, ln_g, ln_b, m_w_in, m_conv_w, m_conv_b, m_a_log, m_dt_bias, m_d, m_norm_w, m_w_out, r_w_in, r_mu, r_w0, r_w_up, r_a0, r_a_up, r_k_k, r_k_a, r_r_k, r_ln_w, r_ln_b, r_w_out, a_w_in, a_q_norm, a_k_norm, a_w_out):
    weights = (ln_g, ln_b, m_w_in, m_conv_w, m_conv_b, m_a_log, m_dt_bias, m_d, m_norm_w, m_w_out,
               r_w_in, r_mu, r_w0, r_w_up, r_a0, r_a_up, r_k_k, r_k_a, r_r_k, r_ln_w, r_ln_b, r_w_out,
               a_w_in, a_q_norm, a_k_norm, a_w_out)
    return (_trunk(x_prompt, *weights), _trunk(x_sample, *weights))
```

```python
import functools
import math

import jax
import jax.numpy as jnp
from jax import lax
from jax.experimental import pallas as pl
from jax.experimental.pallas import tpu as pltpu

F32 = jnp.float32
BF16 = jnp.bfloat16

D_MODEL = 1024
DEPTH = 4
N_MIXERS = 3
GRID_W = 64
LN_EPS = 1e-5

M_DI = 2048
M_HEADDIM = 64
M_HEADS = 32
M_STATE = 128
M_GROUPS = 8
M_HPG = M_HEADS // M_GROUPS
M_GW = M_HPG * M_HEADDIM
M_CONV = 5
M_CHUNK = 128
M_CONV_CH = M_DI + 2 * M_GROUPS * M_STATE
M_NORM_EPS = 1e-5
M_DT_PAD = 128

R_HEADSIZE = 64
R_DIM = 1024
R_HEADS = 16
R_LORA = 64
R_IN = 4 * R_DIM + 4 * R_LORA
R_GN_EPS = 64e-5
R_CHUNK = 64
R_HEAD_GROUP = 8

A_HEADDIM = 128
A_HEADS = 8
A_KV_HEADS = 2
A_GROUP = A_HEADS // A_KV_HEADS
A_DIM = A_HEADS * A_HEADDIM
A_KV_DIM = A_KV_HEADS * A_HEADDIM
ROPE_AXIS_DIM = A_HEADDIM // 2
ROPE_THETA = 10000.0
QK_NORM_EPS = 1e-6
A_TQ = 1024
A_SPLIT = 8

SUBLANES = 8
VMEM_LIMIT = 56 * 1024 * 1024


def _cparams(sem):
    return pltpu.CompilerParams(dimension_semantics=sem, vmem_limit_bytes=VMEM_LIMIT)


def _dot(a, b):
    return jnp.dot(a, b, preferred_element_type=F32)


def _dot_nt(a, b):
    return lax.dot_general(a, b, (((1,), (1,)), ((), ())), preferred_element_type=F32)


def _dot_tn(a, b):
    return lax.dot_general(a, b, (((0,), (0,)), ((), ())), preferred_element_type=F32)


def _split3(x):
    hi = x.astype(BF16)
    r1 = x - hi.astype(F32)
    mid = r1.astype(BF16)
    lo = (r1 - mid.astype(F32)).astype(BF16)
    return hi, mid, lo


def _tri_cumsum(tri, x):
    hi, mid, lo = _split3(x)
    return _dot(tri, hi) + _dot(tri, mid) + _dot(tri, lo)


def _dot_f32ish(a, b):
    ah = a.astype(BF16)
    al = (a - ah.astype(F32)).astype(BF16)
    bh = b.astype(BF16)
    bl = (b - bh.astype(F32)).astype(BF16)
    return _dot(ah, bh) + _dot(ah, bl) + _dot(al, bh)


def _silu(x):
    return x * (1.0 / (1.0 + jnp.exp(-x)))


def _softplus(x):
    return jnp.maximum(x, 0.0) + jnp.log(1.0 + jnp.exp(-jnp.abs(x)))


def _proj_kernel(x_ref, w_ref, o_ref):
    o_ref[...] = _dot(x_ref[...].astype(BF16), w_ref[...])


def _proj(x, w, tm=1024, tn=None):
    t, k = x.shape
    n = w.shape[1]
    if tn is None:
        tn = n if n <= 1024 else 1024
    assert t % tm == 0 and n % tn == 0
    return pl.pallas_call(
        _proj_kernel,
        grid=(t // tm, n // tn),
        in_specs=[pl.BlockSpec((tm, k), lambda i, j: (i, 0)),
                  pl.BlockSpec((k, tn), lambda i, j: (0, j))],
        out_specs=pl.BlockSpec((tm, tn), lambda i, j: (i, j)),
        out_shape=jax.ShapeDtypeStruct((t, n), F32),
        compiler_params=_cparams(("parallel", "arbitrary")),
        name="proj",
    )(x, w)


def _out_ln_kernel(alpha, y_ref, w_ref, x_ref, g_ref, b_ref, o_ref):
    h = _dot(y_ref[...].astype(BF16), w_ref[...])
    s = alpha * x_ref[...] + h
    mu = jnp.mean(s, axis=-1, keepdims=True)
    d = s - mu
    var = jnp.mean(d * d, axis=-1, keepdims=True)
    o_ref[...] = d * lax.rsqrt(var + LN_EPS) * g_ref[...] + b_ref[...]


def _out_ln(y, w, x, g, b, alpha, tm=512):
    t, k = y.shape
    d = w.shape[1]
    assert t % tm == 0
    return pl.pallas_call(
        functools.partial(_out_ln_kernel, alpha),
        grid=(t // tm,),
        in_specs=[pl.BlockSpec((tm, k), lambda i: (i, 0)),
                  pl.BlockSpec((k, d), lambda i: (0, 0)),
                  pl.BlockSpec((tm, d), lambda i: (i, 0)),
                  pl.BlockSpec((1, d), lambda i: (0, 0)),
                  pl.BlockSpec((1, d), lambda i: (0, 0))],
        out_specs=pl.BlockSpec((tm, d), lambda i: (i, 0)),
        out_shape=jax.ShapeDtypeStruct((t, d), F32),
        compiler_params=_cparams(("parallel",)),
        name="out_ln",
    )(y, w, x, g.reshape(1, d), b.reshape(1, d))


def _conv_kernel(nblk, tl, prev_ref, cur_ref, next_ref, w_ref, b_ref, o_ref, ext_ref):
    i = pl.program_id(1)
    halo = SUBLANES
    ext_ref[pl.ds(0, halo), :] = jnp.where(i > 0, prev_ref[0], 0.0)
    ext_ref[pl.ds(halo, tl), :] = cur_ref[0]
    ext_ref[pl.ds(halo + tl, halo), :] = jnp.where(i < nblk - 1, next_ref[0], 0.0)
    pad = M_CONV // 2
    acc = b_ref[...] + w_ref[0:1, :] * ext_ref[pl.ds(halo - pad, tl), :]
    for k in range(1, M_CONV):
        acc = acc + w_ref[k:k + 1, :] * ext_ref[pl.ds(halo - pad + k, tl), :]
    o_ref[0] = _silu(acc)


def _conv_silu(xbc, w, b, tl=512, tc=1024):
    bsz, seqlen, c = xbc.shape
    nblk = seqlen // tl
    r = tl // SUBLANES
    nrow = seqlen // SUBLANES
    return pl.pallas_call(
        functools.partial(_conv_kernel, nblk, tl),
        grid=(bsz, nblk, c // tc),
        in_specs=[
            pl.BlockSpec((1, SUBLANES, tc), lambda bb, i, j: (bb, jnp.maximum(i * r - 1, 0), j)),
            pl.BlockSpec((1, tl, tc), lambda bb, i, j: (bb, i, j)),
            pl.BlockSpec((1, SUBLANES, tc), lambda bb, i, j: (bb, jnp.minimum((i + 1) * r, nrow - 1), j)),
            pl.BlockSpec((M_CONV, tc), lambda bb, i, j: (0, j)),
            pl.BlockSpec((1, tc), lambda bb, i, j: (0, j)),
        ],
        out_specs=pl.BlockSpec((1, tl, tc), lambda bb, i, j: (bb, i, j)),
        out_shape=jax.ShapeDtypeStruct((bsz, seqlen, c), F32),
        scratch_shapes=[pltpu.VMEM((tl + 2 * SUBLANES, tc), F32)],
        compiler_params=_cparams(("parallel", "parallel", "parallel")),
        name="conv_silu",
    )(xbc, xbc, xbc, w, b.reshape(1, c))


def _ssd_kernel(rev, final, *refs):
    if final:
        (x_ref, bc_ref, dt_ref, dtb_ref, alog_ref, z_ref, yb_ref, dsk_ref, nw_ref,
         o_ref, h_ref, y_ref) = refs
    else:
        x_ref, bc_ref, dt_ref, dtb_ref, alog_ref, o_ref, h_ref = refs
        y_ref = o_ref.at[0]
    q = M_CHUNK
    c = pl.program_id(1)

    @pl.when(c == 0)
    def _():
        h_ref[...] = jnp.zeros_like(h_ref)

    d = 1 if rev else 0
    x = x_ref[0]
    dt = _softplus(dt_ref[0] + dtb_ref[...])
    dta = dt * (-jnp.exp(alog_ref[...]))
    row = lax.broadcasted_iota(jnp.int32, (q, q), 0)
    col = lax.broadcasted_iota(jnp.int32, (q, q), 1)
    mask = (col >= row) if rev else (col <= row)
    acs = _tri_cumsum(jnp.where(mask, 1.0, 0.0).astype(BF16), dta)
    end = acs[0:1, :] if rev else acs[q - 1:q, :]
    cdec = jnp.exp(end)
    acs_t = acs.T
    dt_t = dt.T
    e_in_t = (dt * jnp.exp(end - acs)).T
    first = col < M_HEADDIM

    cms, cbs, bts = [], [], []
    for g in range(M_GROUPS):
        bm = bc_ref[0, :, g * M_STATE:(g + 1) * M_STATE]
        cm = bc_ref[0, :, (M_GROUPS + g) * M_STATE:(M_GROUPS + g + 1) * M_STATE]
        cms.append(cm)
        cbs.append(_dot_nt(cm.astype(BF16), bm.astype(BF16)))
        bts.append(bm.T)

    for pr in range(M_HEADS // 2):
        g = pr // (M_HPG // 2)
        tile = slice(pr * 2 * M_HEADDIM, (pr + 1) * 2 * M_HEADDIM)
        xp = x[:, tile].astype(BF16)
        hp = h_ref[pr]
        rhs = jnp.concatenate([xp, hp.astype(BF16)], axis=0)
        ys, sts = [], []
        for r in range(2):
            j = d * M_HEADS + 2 * pr + r
            colb = jnp.broadcast_to(acs[:, j:j + 1], (q, q))
            dec = jnp.exp(jnp.where(mask, colb - acs_t[j:j + 1, :], -jnp.inf)) * dt_t[j:j + 1, :]
            sc = (cbs[g] * dec).astype(BF16)
            ce = (cms[g] * jnp.exp(colb)).astype(BF16)
            ys.append(_dot(jnp.concatenate([sc, ce], axis=1), rhs))
            sts.append(_dot((bts[g] * e_in_t[j:j + 1, :]).astype(BF16), xp))
        y_ref[:, tile] = jnp.where(first, ys[0], ys[1])
        j0 = d * M_HEADS + 2 * pr
        cd = jnp.where(first[0:1, :], cdec[:, j0:j0 + 1], cdec[:, j0 + 1:j0 + 2])
        h_ref[pr] = hp * cd + jnp.where(first, sts[0], sts[1])

    if final:
        y = (y_ref[...] + yb_ref[0] + x * dsk_ref[...]) * _silu(z_ref[0])
        gw = M_DI // M_GROUPS
        for g in range(M_GROUPS):
            yg = y[:, g * gw:(g + 1) * gw]
            ms = jnp.mean(yg * yg, axis=-1, keepdims=True)
            o_ref[0, :, g * gw:(g + 1) * gw] = yg * lax.rsqrt(ms + M_NORM_EPS) * nw_ref[:, g * gw:(g + 1) * gw]


def _ssd_pass(rev, final, xa, dtr, dtb, alog, extra=()):
    bsz, seqlen, _ = xa.shape
    nc = seqlen // M_CHUNK
    ci = (lambda c: nc - 1 - c) if rev else (lambda c: c)
    tok = lambda width, blk: pl.BlockSpec((1, M_CHUNK, width), lambda b, c: (b, ci(c), blk))
    par = lambda width: pl.BlockSpec((1, width), lambda b, c: (0, 0))
    in_specs = [tok(M_DI, 0), tok(M_DI, 1), tok(M_DT_PAD, 0), par(M_DT_PAD), par(M_DT_PAD)]
    scratch = [pltpu.VMEM((M_HEADS // 2, M_STATE, 2 * M_HEADDIM), F32)]
    if final:
        in_specs += [tok(M_DI, 0), tok(M_DI, 0), par(M_DI), par(M_DI)]
        scratch.append(pltpu.VMEM((M_CHUNK, M_DI), F32))
    return pl.pallas_call(
        functools.partial(_ssd_kernel, rev, final),
        grid=(bsz, nc),
        in_specs=in_specs,
        out_specs=tok(M_DI, 0),
        out_shape=jax.ShapeDtypeStruct((bsz, seqlen, M_DI), F32),
        scratch_shapes=scratch,
        compiler_params=_cparams(("parallel", "arbitrary")),
        name="ssd_final" if final else "ssd_bwd",
    )(xa, xa, dtr, dtb, alog, *extra)


def _mamba_layer(x, bsz, seqlen, w_in, conv_w, conv_b, a_log, dt_bias, d_skip, norm_w, w_out, ln_g, ln_b, alpha):
    t = bsz * seqlen
    wz = w_in[:, :M_DI].astype(BF16)
    wx = w_in[:, M_DI:M_DI + M_CONV_CH].astype(BF16)
    wdt = jnp.pad(w_in[:, M_DI + M_CONV_CH:], ((0, 0), (0, M_DT_PAD - 2 * M_HEADS))).astype(BF16)
    z = _proj(x, wz)
    xbc = _proj(x, wx)
    dtr = _proj(x, wdt).reshape(bsz, seqlen, M_DT_PAD)
    xa = _conv_silu(xbc.reshape(bsz, seqlen, M_CONV_CH), conv_w, conv_b)
    pad = (0, M_DT_PAD - 2 * M_HEADS)
    dtb = jnp.pad(dt_bias.reshape(-1), pad).reshape(1, M_DT_PAD)
    alog = jnp.pad(a_log.reshape(-1), pad).reshape(1, M_DT_PAD)
    y_b = _ssd_pass(True, False, xa, dtr, dtb, alog)
    dsk = jnp.repeat(d_skip, M_HEADDIM).reshape(1, M_DI)
    yn = _ssd_pass(False, True, xa, dtr, dtb, alog,
                   extra=(z.reshape(bsz, seqlen, M_DI), y_b, dsk, norm_w.reshape(1, M_DI)))
    return _out_ln(yn.reshape(t, M_DI), w_out.astype(BF16), x, ln_g, ln_b, alpha)


def _rwkv_kernel(rev, final, nc, *refs):
    if final:
        (prev_ref, u_ref, next_ref, mu_ref, w0_ref, wup_ref, a0_ref, aup_ref, kk_ref, ka_ref,
         a0b_ref, aupb_ref, rk_ref, lnw_ref, lnb_ref, yb_ref, o_ref, s_ref, ext_ref, y_ref) = refs
    else:
        (prev_ref, u_ref, next_ref, mu_ref, w0_ref, wup_ref, a0_ref, aup_ref, kk_ref, ka_ref,
         o_ref, s_ref, ext_ref) = refs
        y_ref = o_ref.at[0]
    q = R_CHUNK
    n = R_HEADSIZE
    c = pl.program_id(1)
    cc = (nc - 1 - c) if rev else c

    @pl.when(c == 0)
    def _():
        s_ref[...] = jnp.zeros_like(s_ref)

    halo = SUBLANES
    ext_ref[pl.ds(0, halo), :] = jnp.where(cc > 0, prev_ref[0], 0.0)
    ext_ref[pl.ds(halo, q), :] = u_ref[0]
    ext_ref[pl.ds(halo + q, halo), :] = jnp.where(cc < nc - 1, next_ref[0], 0.0)
    u0 = u_ref[0]
    shifted = 0.5 * (ext_ref[pl.ds(halo - 1, q), :] + ext_ref[pl.ds(halo + 1, q), :])
    u = u0 + (shifted - u0) * mu_ref[...]

    d = 1 if rev else 0
    r_all = u[:, 0:R_DIM]
    k_all = u[:, R_DIM:2 * R_DIM]
    v_all = u[:, 2 * R_DIM:3 * R_DIM]
    lw = u[:, 4 * R_DIM + d * R_LORA:4 * R_DIM + (d + 1) * R_LORA]
    la = u[:, 4 * R_DIM + (2 + d) * R_LORA:4 * R_DIM + (3 + d) * R_LORA]
    w_log = -_softplus(-(w0_ref[...] + _dot_f32ish(jnp.tanh(lw), wup_ref[...]))) - 0.5
    ld = -jnp.exp(w_log)
    a = 1.0 / (1.0 + jnp.exp(-(a0_ref[...] + _dot_f32ish(la, aup_ref[...]))))
    kkr = k_all * kk_ref[...]
    k_dir = k_all * (1.0 + (a - 1.0) * ka_ref[...])

    row = lax.broadcasted_iota(jnp.int32, (q, q), 0)
    col = lax.broadcasted_iota(jnp.int32, (q, q), 1)
    incl = (col >= row) if rev else (col <= row)
    strict = (col > row) if rev else (col < row)
    cum = _tri_cumsum(jnp.where(incl, 1.0, 0.0).astype(BF16), ld)
    w_in = jnp.exp(cum)
    w_inv = jnp.exp(-cum)
    w_ex = jnp.exp(cum - ld)
    w_end = w_in[0:1, :] if rev else w_in[q - 1:q, :]
    eye = jnp.where(row == col, 1.0, 0.0)
    row2 = lax.broadcasted_iota(jnp.int32, (q, 2 * q), 0)
    lane2 = lax.broadcasted_iota(jnp.int32, (q, 2 * q), 1)
    col2 = lane2 & (q - 1)
    k_half = lane2 >= q
    incl2 = (col2 >= row2) if rev else (col2 <= row2)
    strict_k2 = ((col2 > row2) if rev else (col2 < row2)) & k_half
    zeros_v = jnp.zeros((q, n), BF16)
    nlev = int(math.log2(q))

    for h0 in range(0, R_HEADS, R_HEAD_GROUP):
        heads = range(h0, h0 + R_HEAD_GROUP)
        sls = [slice(h * n, (h + 1) * n) for h in heads]
        l1s, r1s, r1es, vbs = [], [], [], []
        for sl in sls:
            kk = kkr[:, sl]
            kk = kk / jnp.maximum(jnp.sqrt(jnp.sum(kk * kk, axis=-1, keepdims=True)), 1e-12)
            rt = r_all[:, sl] * w_in[:, sl]
            kt = k_dir[:, sl] * w_inv[:, sl]
            bt = kk * a[:, sl] * w_inv[:, sl]
            kap = kk * w_ex[:, sl]
            we = w_end[:, sl]
            vbs.append(v_all[:, sl].astype(BF16))
            l1s.append(jnp.concatenate([kap, rt], axis=0).astype(BF16))
            r1s.append(jnp.concatenate([bt, kt], axis=0).astype(BF16))
            r1es.append(jnp.concatenate([-(bt * we), kt * we], axis=0).astype(BF16))
        gms = [_dot_nt(l1, r1) for l1, r1 in zip(l1s, r1s)]
        s0s = [s_ref[h] for h in heads]
        lhs = [_dot_nt(l1, s0.astype(BF16)) for l1, s0 in zip(l1s, s0s)]
        ps = [jnp.where(strict, -gm[0:q, 0:q], 0.0) for gm in gms]
        ts = [eye + p for p in ps]
        pbs = [p.astype(BF16) for p in ps]
        ps = [_dot(pb, pb) for pb in pbs]
        for lvl in range(1, nlev):
            pbs = [p.astype(BF16) for p in ps]
            if lvl < nlev - 1:
                boths = [_dot(jnp.concatenate([t, p], axis=0).astype(BF16), pb) for t, p, pb in zip(ts, ps, pbs)]
                ts = [t + both[0:q] for t, both in zip(ts, boths)]
                ps = [both[q:2 * q] for both in boths]
            else:
                ts = [t + _dot(t.astype(BF16), pb) for t, pb in zip(ts, pbs)]
        tops = [jnp.where(strict_k2, gm[0:q, :], 0.0).astype(BF16) for gm in gms]
        xms = [lh[0:q] + _dot(top, jnp.concatenate([zeros_v, vb], axis=0)) for lh, top, vb in zip(lhs, tops, vbs)]
        ums = [_dot(t.astype(BF16), xm.astype(BF16)).astype(BF16) for t, xm in zip(ts, xms)]
        uvs = [jnp.concatenate([um, vb], axis=0) for um, vb in zip(ums, vbs)]
        bots = [jnp.where(incl2, jnp.where(k_half, gm[q:2 * q, :], -gm[q:2 * q, :]), 0.0).astype(BF16) for gm in gms]
        for sl, lh, bot, uv in zip(sls, lhs, bots, uvs):
            y_ref[:, sl] = lh[q:2 * q] + _dot(bot, uv)
        for h, sl, s0, uv, r1e in zip(heads, sls, s0s, uvs, r1es):
            s_ref[h] = s0 * w_end[:, sl] + _dot_tn(uv, r1e)

    if final:
        a_b = 1.0 / (1.0 + jnp.exp(-(a0b_ref[...] + _dot_f32ish(
            u[:, 4 * R_DIM + 3 * R_LORA:4 * R_DIM + 4 * R_LORA], aupb_ref[...]))))
        k_sum = k_dir + k_all * (1.0 + (a_b - 1.0) * ka_ref[...])
        rkk = r_all * k_sum * rk_ref[...]
        g = _silu(u[:, 3 * R_DIM:4 * R_DIM])
        ytot = y_ref[...] + yb_ref[0]
        for h in range(R_HEADS):
            sl = slice(h * n, (h + 1) * n)
            yh = ytot[:, sl]
            mu_y = jnp.mean(yh, axis=-1, keepdims=True)
            dy = yh - mu_y
            var_y = jnp.mean(dy * dy, axis=-1, keepdims=True)
            bonus = jnp.sum(rkk[:, sl], axis=-1, keepdims=True) * v_all[:, sl]
            o_ref[0, :, sl] = (dy * lax.rsqrt(var_y + R_GN_EPS) * lnw_ref[:, sl] + lnb_ref[:, sl] + bonus) * g[:, sl]


def _rwkv_pass(rev, final, u, params, extra=()):
    bsz, seqlen, width = u.shape
    nc = seqlen // R_CHUNK
    r = R_CHUNK // SUBLANES
    nrow = seqlen // SUBLANES
    ci = (lambda c: nc - 1 - c) if rev else (lambda c: c)
    in_specs = [
        pl.BlockSpec((1, SUBLANES, width), lambda b, c: (b, jnp.maximum(ci(c) * r - 1, 0), 0)),
        pl.BlockSpec((1, R_CHUNK, width), lambda b, c: (b, ci(c), 0)),
        pl.BlockSpec((1, SUBLANES, width), lambda b, c: (b, jnp.minimum((ci(c) + 1) * r, nrow - 1), 0)),
    ]
    for p in params:
        in_specs.append(pl.BlockSpec(p.shape, lambda b, c: (0, 0)))
    scratch = [pltpu.VMEM((R_HEADS, R_HEADSIZE, R_HEADSIZE), F32),
               pltpu.VMEM((R_CHUNK + 2 * SUBLANES, width), F32)]
    if final:
        for p in extra[:-1]:
            in_specs.append(pl.BlockSpec(p.shape, lambda b, c: (0, 0)))
        in_specs.append(pl.BlockSpec((1, R_CHUNK, R_DIM), lambda b, c: (b, ci(c), 0)))
        scratch.append(pltpu.VMEM((R_CHUNK, R_DIM), F32))
    return pl.pallas_call(
        functools.partial(_rwkv_kernel, rev, final, nc),
        grid=(bsz, nc),
        in_specs=in_specs,
        out_specs=pl.BlockSpec((1, R_CHUNK, R_DIM), lambda b, c: (b, ci(c), 0)),
        out_shape=jax.ShapeDtypeStruct((bsz, seqlen, R_DIM), F32),
        scratch_shapes=scratch,
        compiler_params=_cparams(("parallel", "arbitrary")),
        name="rwkv_final" if final else "rwkv_bwd",
    )(u, u, u, *params, *extra)


def _rwkv_layer(x, bsz, seqlen, w_in, mu, w0, w_up, a0, a_up, k_k, k_a, r_k, ln_w, ln_b, w_out, ln_g, ln_bb, alpha):
    t = bsz * seqlen
    u = _proj(x, w_in.astype(BF16), tn=R_IN // 2).reshape(bsz, seqlen, R_IN)
    row = lambda p: p.reshape(1, -1)
    common = lambda d: (row(mu), row(w0[d]), w_up[d], row(a0[d]), a_up[d], row(k_k), row(k_a))
    y_b = _rwkv_pass(True, False, u, common(1))
    out = _rwkv_pass(False, True, u, common(0),
                     extra=(row(a0[1]), a_up[1], row(r_k), row(ln_w), row(ln_b), y_b))
    return _out_ln(out.reshape(t, R_DIM), w_out.astype(BF16), x, ln_g, ln_bb, alpha)


def _rope_tables(seqlen):
    rows = seqlen // GRID_W
    rowp = jnp.repeat(jnp.arange(rows, dtype=F32), GRID_W)
    colp = jnp.tile(jnp.arange(GRID_W, dtype=F32), rows)
    inv = ROPE_THETA ** (-jnp.arange(0, ROPE_AXIS_DIM, 2, dtype=F32) / ROPE_AXIS_DIM)
    ar = rowp[:, None] * inv[None]
    ac = colp[:, None] * inv[None]
    cos = jnp.concatenate([jnp.cos(ar), jnp.cos(ar), jnp.cos(ac), jnp.cos(ac)], axis=-1)
    sin = jnp.concatenate([-jnp.sin(ar), jnp.sin(ar), -jnp.sin(ac), jnp.sin(ac)], axis=-1)
    return cos, sin


def _qk_prep_kernel(scale, q_ref, kv_ref, cos_ref, sin_ref, qn_ref, kn_ref, qo_ref, ko_ref, vo_ref):
    cos = cos_ref[...]
    sin = sin_ref[...]
    lane = lax.broadcasted_iota(jnp.int32, cos.shape, 1)
    half = ROPE_AXIS_DIM // 2
    low = (lane & half) == 0

    def norm_rope(xh, gain, mult):
        ms = jnp.mean(xh * xh, axis=-1, keepdims=True)
        xn = xh * lax.rsqrt(ms + QK_NORM_EPS) * gain
        partner = jnp.where(low, pltpu.roll(xn, A_HEADDIM - half, 1), pltpu.roll(xn, half, 1))
        return (xn * cos + partner * sin) * mult

    for h in range(A_HEADS):
        sl = slice(h * A_HEADDIM, (h + 1) * A_HEADDIM)
        qo_ref[0, :, sl] = norm_rope(q_ref[0, :, sl], qn_ref[...], scale).astype(BF16)
    for h in range(A_KV_HEADS):
        sl = slice(h * A_HEADDIM, (h + 1) * A_HEADDIM)
        ko_ref[0, :, sl] = norm_rope(kv_ref[0, :, sl], kn_ref[...], 1.0).astype(BF16)
    vo_ref[0] = kv_ref[0, :, A_KV_DIM:2 * A_KV_DIM].astype(BF16)


def _attn_kernel(q_ref, k_ref, v_ref, g_ref, o_ref):
    rows = q_ref.shape[1] // A_SPLIT
    k = k_ref[0]
    v = v_ref[0]
    blocks = [slice(i * rows, (i + 1) * rows) for i in range(A_SPLIT)]
    s_next = _dot_nt(q_ref[0, blocks[0], :], k)
    for i, b in enumerate(blocks):
        s = s_next
        if i + 1 < A_SPLIT:
            s_next = _dot_nt(q_ref[0, blocks[i + 1], :], k)
        m = jnp.max(s, axis=-1, keepdims=True)
        p = jnp.exp(s - m)
        l = jnp.sum(p, axis=-1, keepdims=True)
        o_ref[0, b, :] = (_dot(p.astype(BF16), v) / l) * _silu(g_ref[0, b, :])


def _attn_layer(x, bsz, seqlen, w_in, q_norm, k_norm, w_out, ln_g, ln_b, alpha):
    t = bsz * seqlen
    wq = w_in[:, :A_DIM].astype(BF16)
    wkv = w_in[:, A_DIM:A_DIM + 2 * A_KV_DIM].astype(BF16)
    wg = w_in[:, A_DIM + 2 * A_KV_DIM:].astype(BF16)
    q = _proj(x, wq).reshape(bsz, seqlen, A_DIM)
    kv = _proj(x, wkv).reshape(bsz, seqlen, 2 * A_KV_DIM)
    gate = _proj(x, wg).reshape(bsz, seqlen, A_DIM)
    cos, sin = _rope_tables(seqlen)
    tl = 512
    tok = lambda width: pl.BlockSpec((1, tl, width), lambda b, i: (b, i, 0))
    tab = pl.BlockSpec((tl, A_HEADDIM), lambda b, i: (i, 0))
    par = pl.BlockSpec((1, A_HEADDIM), lambda b, i: (0, 0))
    qr, kr, vb = pl.pallas_call(
        functools.partial(_qk_prep_kernel, A_HEADDIM ** -0.5),
        grid=(bsz, seqlen // tl),
        in_specs=[tok(A_DIM), tok(2 * A_KV_DIM), tab, tab, par, par],
        out_specs=[tok(A_DIM), tok(A_KV_DIM), tok(A_KV_DIM)],
        out_shape=[jax.ShapeDtypeStruct((bsz, seqlen, A_DIM), BF16),
                   jax.ShapeDtypeStruct((bsz, seqlen, A_KV_DIM), BF16),
                   jax.ShapeDtypeStruct((bsz, seqlen, A_KV_DIM), BF16)],
        compiler_params=_cparams(("parallel", "parallel")),
        name="qk_prep",
    )(q, kv, cos, sin, q_norm.reshape(1, A_HEADDIM), k_norm.reshape(1, A_HEADDIM))
    tq = A_TQ
    o = pl.pallas_call(
        _attn_kernel,
        grid=(bsz, A_HEADS, seqlen // tq),
        in_specs=[pl.BlockSpec((1, tq, A_HEADDIM), lambda b, h, i: (b, i, h)),
                  pl.BlockSpec((1, seqlen, A_HEADDIM), lambda b, h, i: (b, 0, h // A_GROUP)),
                  pl.BlockSpec((1, seqlen, A_HEADDIM), lambda b, h, i: (b, 0, h // A_GROUP)),
                  pl.BlockSpec((1, tq, A_HEADDIM), lambda b, h, i: (b, i, h))],
        out_specs=pl.BlockSpec((1, tq, A_HEADDIM), lambda b, h, i: (b, i, h)),
        out_shape=jax.ShapeDtypeStruct((bsz, seqlen, A_DIM), F32),
        compiler_params=_cparams(("parallel", "parallel", "arbitrary")),
        name="attn",
    )(qr, kr, vb, gate)
    return _out_ln(o.reshape(t, A_DIM), w_out.astype(BF16), x, ln_g, ln_b, alpha)


def _trunk(x3, ln_g, ln_b, m_w_in, m_conv_w, m_conv_b, m_a_log, m_dt_bias, m_d, m_norm_w, m_w_out,
           r_w_in, r_mu, r_w0, r_w_up, r_a0, r_a_up, r_k_k, r_k_a, r_r_k, r_ln_w, r_ln_b, r_w_out,
           a_w_in, a_q_norm, a_k_norm, a_w_out):
    bsz, seqlen, _ = x3.shape
    alpha = (2.0 * DEPTH) ** 0.25
    x = x3.reshape(bsz * seqlen, D_MODEL)
    for i in range(DEPTH):
        j = i // N_MIXERS
        kind = i % N_MIXERS
        if kind == 0:
            x = _mamba_layer(x, bsz, seqlen, m_w_in[j], m_conv_w[j], m_conv_b[j], m_a_log[j], m_dt_bias[j],
                             m_d[j], m_norm_w[j], m_w_out[j], ln_g[i], ln_b[i], alpha)
        elif kind == 1:
            x = _rwkv_layer(x, bsz, seqlen, r_w_in[j], r_mu[j], r_w0[j], r_w_up[j], r_a0[j], r_a_up[j],
                            r_k_k[j], r_k_a[j], r_r_k[j].reshape(-1), r_ln_w[j], r_ln_b[j], r_w_out[j],
                            ln_g[i], ln_b[i], alpha)
        else:
            x = _attn_layer(x, bsz, seqlen, a_w_in[j], a_q_norm[j], a_k_norm[j], a_w_out[j],
                            ln_g[i], ln_b[i], alpha)
    return x.reshape(bsz, seqlen, D_MODEL)


def kernel(x_prompt, x_sample, ln_g, ln_b, m_w_in, m_conv_w, m_conv_b, m_a_log, m_dt_bias, m_d, m_norm_w, m_w_out, r_w_in, r_mu, r_w0, r_w_up, r_a0, r_a_up, r_k_k, r_k_a, r_r_k, r_ln_w, r_ln_b, r_w_out, a_w_in, a_q_norm, a_k_norm, a_w_out):
    weights = (ln_g, ln_b, m_w_in, m_conv_w, m_conv_b, m_a_log, m_dt_bias, m_d, m_norm_w, m_w_out,
               r_w_in, r_mu, r_w0, r_w_up, r_a0, r_a_up, r_k_k, r_k_a, r_r_k, r_ln_w, r_ln_b, r_w_out,
               a_w_in, a_q_norm, a_k_norm, a_w_out)
    return (_trunk(x_prompt, *weights), _trunk(x_sample, *weights))
```

```python
import functools
import math

import jax
import jax.numpy as jnp
from jax import lax
from jax.experimental import pallas as pl
from jax.experimental.pallas import tpu as pltpu

F32 = jnp.float32
BF16 = jnp.bfloat16

D_MODEL = 1024
DEPTH = 4
N_MIXERS = 3
GRID_W = 64
LN_EPS = 1e-5

M_DI = 2048
M_HEADDIM = 64
M_HEADS = 32
M_STATE = 128
M_GROUPS = 8
M_HPG = M_HEADS // M_GROUPS
M_GW = M_HPG * M_HEADDIM
M_CONV = 5
M_CHUNK = 128
M_CONV_CH = M_DI + 2 * M_GROUPS * M_STATE
M_NORM_EPS = 1e-5
M_DT_PAD = 128

R_HEADSIZE = 64
R_DIM = 1024
R_HEADS = 16
R_LORA = 64
R_IN = 4 * R_DIM + 4 * R_LORA
R_GN_EPS = 64e-5
R_CHUNK = 64
R_HEAD_GROUP = 16
R_HEAD_GROUP_FINAL = 8

A_HEADDIM = 128
A_HEADS = 8
A_KV_HEADS = 2
A_GROUP = A_HEADS // A_KV_HEADS
A_DIM = A_HEADS * A_HEADDIM
A_KV_DIM = A_KV_HEADS * A_HEADDIM
ROPE_AXIS_DIM = A_HEADDIM // 2
ROPE_THETA = 10000.0
QK_NORM_EPS = 1e-6
A_TQ = 1024
A_SPLIT = 8

SUBLANES = 8
VMEM_LIMIT = 56 * 1024 * 1024


def _cparams(sem):
    return pltpu.CompilerParams(dimension_semantics=sem, vmem_limit_bytes=VMEM_LIMIT)


def _dot(a, b):
    return jnp.dot(a, b, preferred_element_type=F32)


def _dot_nt(a, b):
    return lax.dot_general(a, b, (((1,), (1,)), ((), ())), preferred_element_type=F32)


def _dot_tn(a, b):
    return lax.dot_general(a, b, (((0,), (0,)), ((), ())), preferred_element_type=F32)


def _split3(x):
    hi = x.astype(BF16)
    r1 = x - hi.astype(F32)
    mid = r1.astype(BF16)
    lo = (r1 - mid.astype(F32)).astype(BF16)
    return hi, mid, lo


def _tri_cumsum(tri, x):
    hi, mid, lo = _split3(x)
    return _dot(tri, hi) + _dot(tri, mid) + _dot(tri, lo)


def _dot_f32ish(a, b):
    ah = a.astype(BF16)
    al = (a - ah.astype(F32)).astype(BF16)
    bh = b.astype(BF16)
    bl = (b - bh.astype(F32)).astype(BF16)
    return _dot(ah, bh) + _dot(ah, bl) + _dot(al, bh)


def _silu(x):
    return x * (1.0 / (1.0 + jnp.exp(-x)))


def _softplus(x):
    return jnp.maximum(x, 0.0) + jnp.log(1.0 + jnp.exp(-jnp.abs(x)))


def _proj_kernel(x_ref, w_ref, o_ref):
    o_ref[...] = _dot(x_ref[...].astype(BF16), w_ref[...])


def _proj(x, w, tm=1024, tn=None):
    t, k = x.shape
    n = w.shape[1]
    if tn is None:
        tn = n if n <= 1024 else 1024
    assert t % tm == 0 and n % tn == 0
    return pl.pallas_call(
        _proj_kernel,
        grid=(t // tm, n // tn),
        in_specs=[pl.BlockSpec((tm, k), lambda i, j: (i, 0)),
                  pl.BlockSpec((k, tn), lambda i, j: (0, j))],
        out_specs=pl.BlockSpec((tm, tn), lambda i, j: (i, j)),
        out_shape=jax.ShapeDtypeStruct((t, n), F32),
        compiler_params=_cparams(("parallel", "arbitrary")),
        name="proj",
    )(x, w)


def _out_ln_kernel(alpha, y_ref, w_ref, x_ref, g_ref, b_ref, o_ref):
    h = _dot(y_ref[...].astype(BF16), w_ref[...])
    s = alpha * x_ref[...] + h
    mu = jnp.mean(s, axis=-1, keepdims=True)
    d = s - mu
    var = jnp.mean(d * d, axis=-1, keepdims=True)
    o_ref[...] = d * lax.rsqrt(var + LN_EPS) * g_ref[...] + b_ref[...]


def _out_ln(y, w, x, g, b, alpha, tm=512):
    t, k = y.shape
    d = w.shape[1]
    assert t % tm == 0
    return pl.pallas_call(
        functools.partial(_out_ln_kernel, alpha),
        grid=(t // tm,),
        in_specs=[pl.BlockSpec((tm, k), lambda i: (i, 0)),
                  pl.BlockSpec((k, d), lambda i: (0, 0)),
                  pl.BlockSpec((tm, d), lambda i: (i, 0)),
                  pl.BlockSpec((1, d), lambda i: (0, 0)),
                  pl.BlockSpec((1, d), lambda i: (0, 0))],
        out_specs=pl.BlockSpec((tm, d), lambda i: (i, 0)),
        out_shape=jax.ShapeDtypeStruct((t, d), F32),
        compiler_params=_cparams(("parallel",)),
        name="out_ln",
    )(y, w, x, g.reshape(1, d), b.reshape(1, d))


def _conv_kernel(nblk, tl, prev_ref, cur_ref, next_ref, w_ref, b_ref, o_ref, ext_ref):
    i = pl.program_id(1)
    halo = SUBLANES
    ext_ref[pl.ds(0, halo), :] = jnp.where(i > 0, prev_ref[0], 0.0)
    ext_ref[pl.ds(halo, tl), :] = cur_ref[0]
    ext_ref[pl.ds(halo + tl, halo), :] = jnp.where(i < nblk - 1, next_ref[0], 0.0)
    pad = M_CONV // 2
    acc = b_ref[...] + w_ref[0:1, :] * ext_ref[pl.ds(halo - pad, tl), :]
    for k in range(1, M_CONV):
        acc = acc + w_ref[k:k + 1, :] * ext_ref[pl.ds(halo - pad + k, tl), :]
    o_ref[0] = _silu(acc)


def _conv_silu(xbc, w, b, tl=512, tc=1024):
    bsz, seqlen, c = xbc.shape
    nblk = seqlen // tl
    r = tl // SUBLANES
    nrow = seqlen // SUBLANES
    return pl.pallas_call(
        functools.partial(_conv_kernel, nblk, tl),
        grid=(bsz, nblk, c // tc),
        in_specs=[
            pl.BlockSpec((1, SUBLANES, tc), lambda bb, i, j: (bb, jnp.maximum(i * r - 1, 0), j)),
            pl.BlockSpec((1, tl, tc), lambda bb, i, j: (bb, i, j)),
            pl.BlockSpec((1, SUBLANES, tc), lambda bb, i, j: (bb, jnp.minimum((i + 1) * r, nrow - 1), j)),
            pl.BlockSpec((M_CONV, tc), lambda bb, i, j: (0, j)),
            pl.BlockSpec((1, tc), lambda bb, i, j: (0, j)),
        ],
        out_specs=pl.BlockSpec((1, tl, tc), lambda bb, i, j: (bb, i, j)),
        out_shape=jax.ShapeDtypeStruct((bsz, seqlen, c), F32),
        scratch_shapes=[pltpu.VMEM((tl + 2 * SUBLANES, tc), F32)],
        compiler_params=_cparams(("parallel", "parallel", "parallel")),
        name="conv_silu",
    )(xbc, xbc, xbc, w, b.reshape(1, c))


def _ssd_kernel(rev, final, *refs):
    if final:
        (x_ref, bc_ref, dt_ref, dtb_ref, alog_ref, z_ref, yb_ref, dsk_ref, nw_ref,
         o_ref, h_ref, y_ref) = refs
    else:
        x_ref, bc_ref, dt_ref, dtb_ref, alog_ref, o_ref, h_ref = refs
        y_ref = o_ref.at[0]
    q = M_CHUNK
    c = pl.program_id(1)

    @pl.when(c == 0)
    def _():
        h_ref[...] = jnp.zeros_like(h_ref)

    d = 1 if rev else 0
    x = x_ref[0]
    dt = _softplus(dt_ref[0] + dtb_ref[...])
    dta = dt * (-jnp.exp(alog_ref[...]))
    row = lax.broadcasted_iota(jnp.int32, (q, q), 0)
    col = lax.broadcasted_iota(jnp.int32, (q, q), 1)
    mask = (col >= row) if rev else (col <= row)
    acs = _tri_cumsum(jnp.where(mask, 1.0, 0.0).astype(BF16), dta)
    end = acs[0:1, :] if rev else acs[q - 1:q, :]
    cdec = jnp.exp(end)
    acs_t = acs.T
    dt_t = dt.T
    e_in_t = (dt * jnp.exp(end - acs)).T
    first = col < M_HEADDIM

    cms, cbs, bts = [], [], []
    for g in range(M_GROUPS):
        bm = bc_ref[0, :, g * M_STATE:(g + 1) * M_STATE]
        cm = bc_ref[0, :, (M_GROUPS + g) * M_STATE:(M_GROUPS + g + 1) * M_STATE]
        cms.append(cm)
        cbs.append(_dot_nt(cm.astype(BF16), bm.astype(BF16)))
        bts.append(bm.T)

    for pr in range(M_HEADS // 2):
        g = pr // (M_HPG // 2)
        tile = slice(pr * 2 * M_HEADDIM, (pr + 1) * 2 * M_HEADDIM)
        xp = x[:, tile].astype(BF16)
        hp = h_ref[pr]
        rhs = jnp.concatenate([xp, hp.astype(BF16)], axis=0)
        ys, sts = [], []
        for r in range(2):
            j = d * M_HEADS + 2 * pr + r
            colb = jnp.broadcast_to(acs[:, j:j + 1], (q, q))
            dec = jnp.exp(jnp.where(mask, colb - acs_t[j:j + 1, :], -jnp.inf)) * dt_t[j:j + 1, :]
            sc = (cbs[g] * dec).astype(BF16)
            ce = (cms[g] * jnp.exp(colb)).astype(BF16)
            ys.append(_dot(jnp.concatenate([sc, ce], axis=1), rhs))
            sts.append(_dot((bts[g] * e_in_t[j:j + 1, :]).astype(BF16), xp))
        y_ref[:, tile] = jnp.where(first, ys[0], ys[1])
        j0 = d * M_HEADS + 2 * pr
        cd = jnp.where(first[0:1, :], cdec[:, j0:j0 + 1], cdec[:, j0 + 1:j0 + 2])
        h_ref[pr] = hp * cd + jnp.where(first, sts[0], sts[1])

    if final:
        y = (y_ref[...] + yb_ref[0] + x * dsk_ref[...]) * _silu(z_ref[0])
        gw = M_DI // M_GROUPS
        for g in range(M_GROUPS):
            yg = y[:, g * gw:(g + 1) * gw]
            ms = jnp.mean(yg * yg, axis=-1, keepdims=True)
            o_ref[0, :, g * gw:(g + 1) * gw] = yg * lax.rsqrt(ms + M_NORM_EPS) * nw_ref[:, g * gw:(g + 1) * gw]


def _ssd_pass(rev, final, xa, dtr, dtb, alog, extra=()):
    bsz, seqlen, _ = xa.shape
    nc = seqlen // M_CHUNK
    ci = (lambda c: nc - 1 - c) if rev else (lambda c: c)
    tok = lambda width, blk: pl.BlockSpec((1, M_CHUNK, width), lambda b, c: (b, ci(c), blk))
    par = lambda width: pl.BlockSpec((1, width), lambda b, c: (0, 0))
    in_specs = [tok(M_DI, 0), tok(M_DI, 1), tok(M_DT_PAD, 0), par(M_DT_PAD), par(M_DT_PAD)]
    scratch = [pltpu.VMEM((M_HEADS // 2, M_STATE, 2 * M_HEADDIM), F32)]
    if final:
        in_specs += [tok(M_DI, 0), tok(M_DI, 0), par(M_DI), par(M_DI)]
        scratch.append(pltpu.VMEM((M_CHUNK, M_DI), F32))
    return pl.pallas_call(
        functools.partial(_ssd_kernel, rev, final),
        grid=(bsz, nc),
        in_specs=in_specs,
        out_specs=tok(M_DI, 0),
        out_shape=jax.ShapeDtypeStruct((bsz, seqlen, M_DI), F32),
        scratch_shapes=scratch,
        compiler_params=_cparams(("parallel", "arbitrary")),
        name="ssd_final" if final else "ssd_bwd",
    )(xa, xa, dtr, dtb, alog, *extra)


def _mamba_layer(x, bsz, seqlen, w_in, conv_w, conv_b, a_log, dt_bias, d_skip, norm_w, w_out, ln_g, ln_b, alpha):
    t = bsz * seqlen
    wz = w_in[:, :M_DI].astype(BF16)
    wx = w_in[:, M_DI:M_DI + M_CONV_CH].astype(BF16)
    wdt = jnp.pad(w_in[:, M_DI + M_CONV_CH:], ((0, 0), (0, M_DT_PAD - 2 * M_HEADS))).astype(BF16)
    z = _proj(x, wz)
    xbc = _proj(x, wx)
    dtr = _proj(x, wdt).reshape(bsz, seqlen, M_DT_PAD)
    xa = _conv_silu(xbc.reshape(bsz, seqlen, M_CONV_CH), conv_w, conv_b)
    pad = (0, M_DT_PAD - 2 * M_HEADS)
    dtb = jnp.pad(dt_bias.reshape(-1), pad).reshape(1, M_DT_PAD)
    alog = jnp.pad(a_log.reshape(-1), pad).reshape(1, M_DT_PAD)
    y_b = _ssd_pass(True, False, xa, dtr, dtb, alog)
    dsk = jnp.repeat(d_skip, M_HEADDIM).reshape(1, M_DI)
    yn = _ssd_pass(False, True, xa, dtr, dtb, alog,
                   extra=(z.reshape(bsz, seqlen, M_DI), y_b, dsk, norm_w.reshape(1, M_DI)))
    return _out_ln(yn.reshape(t, M_DI), w_out.astype(BF16), x, ln_g, ln_b, alpha)


def _rwkv_kernel(rev, final, nc, *refs):
    if final:
        (prev_ref, u_ref, next_ref, mu_ref, w0_ref, wup_ref, a0_ref, aup_ref, kk_ref, ka_ref,
         a0b_ref, aupb_ref, rk_ref, lnw_ref, lnb_ref, yb_ref, o_ref, s_ref, ext_ref, y_ref) = refs
    else:
        (prev_ref, u_ref, next_ref, mu_ref, w0_ref, wup_ref, a0_ref, aup_ref, kk_ref, ka_ref,
         o_ref, s_ref, ext_ref) = refs
        y_ref = o_ref.at[0]
    q = R_CHUNK
    n = R_HEADSIZE
    c = pl.program_id(1)
    cc = (nc - 1 - c) if rev else c

    @pl.when(c == 0)
    def _():
        s_ref[...] = jnp.zeros_like(s_ref)

    halo = SUBLANES
    ext_ref[pl.ds(0, halo), :] = jnp.where(cc > 0, prev_ref[0], 0.0)
    ext_ref[pl.ds(halo, q), :] = u_ref[0]
    ext_ref[pl.ds(halo + q, halo), :] = jnp.where(cc < nc - 1, next_ref[0], 0.0)
    u0 = u_ref[0]
    shifted = 0.5 * (ext_ref[pl.ds(halo - 1, q), :] + ext_ref[pl.ds(halo + 1, q), :])
    u = u0 + (shifted - u0) * mu_ref[...]

    d = 1 if rev else 0
    r_all = u[:, 0:R_DIM]
    k_all = u[:, R_DIM:2 * R_DIM]
    v_all = u[:, 2 * R_DIM:3 * R_DIM]
    lw = u[:, 4 * R_DIM + d * R_LORA:4 * R_DIM + (d + 1) * R_LORA]
    la = u[:, 4 * R_DIM + (2 + d) * R_LORA:4 * R_DIM + (3 + d) * R_LORA]
    w_log = -_softplus(-(w0_ref[...] + _dot_f32ish(jnp.tanh(lw), wup_ref[...]))) - 0.5
    ld = -jnp.exp(w_log)
    a = 1.0 / (1.0 + jnp.exp(-(a0_ref[...] + _dot_f32ish(la, aup_ref[...]))))
    kkr = k_all * kk_ref[...]
    k_dir = k_all * (1.0 + (a - 1.0) * ka_ref[...])

    row = lax.broadcasted_iota(jnp.int32, (q, q), 0)
    col = lax.broadcasted_iota(jnp.int32, (q, q), 1)
    incl = (col >= row) if rev else (col <= row)
    strict = (col > row) if rev else (col < row)
    cum = _tri_cumsum(jnp.where(incl, 1.0, 0.0).astype(BF16), ld)
    w_in = jnp.exp(cum)
    w_inv = jnp.exp(-cum)
    w_ex = jnp.exp(cum - ld)
    w_end = w_in[0:1, :] if rev else w_in[q - 1:q, :]
    eye = jnp.where(row == col, 1.0, 0.0)
    row2 = lax.broadcasted_iota(jnp.int32, (q, 2 * q), 0)
    lane2 = lax.broadcasted_iota(jnp.int32, (q, 2 * q), 1)
    col2 = lane2 & (q - 1)
    k_half = lane2 >= q
    incl2 = (col2 >= row2) if rev else (col2 <= row2)
    strict_k2 = ((col2 > row2) if rev else (col2 < row2)) & k_half
    zeros_v = jnp.zeros((q, n), BF16)
    nlev = int(math.log2(q))

    group = R_HEAD_GROUP_FINAL if final else R_HEAD_GROUP
    for h0 in range(0, R_HEADS, group):
        heads = range(h0, h0 + group)
        sls = [slice(h * n, (h + 1) * n) for h in heads]
        l1s, r1s, r1es, vbs = [], [], [], []
        for sl in sls:
            kk = kkr[:, sl]
            kk = kk / jnp.maximum(jnp.sqrt(jnp.sum(kk * kk, axis=-1, keepdims=True)), 1e-12)
            rt = r_all[:, sl] * w_in[:, sl]
            kt = k_dir[:, sl] * w_inv[:, sl]
            bt = kk * a[:, sl] * w_inv[:, sl]
            kap = kk * w_ex[:, sl]
            we = w_end[:, sl]
            vbs.append(v_all[:, sl].astype(BF16))
            l1s.append(jnp.concatenate([kap, rt], axis=0).astype(BF16))
            r1s.append(jnp.concatenate([bt, kt], axis=0).astype(BF16))
            r1es.append(jnp.concatenate([-(bt * we), kt * we], axis=0).astype(BF16))
        gms = [_dot_nt(l1, r1) for l1, r1 in zip(l1s, r1s)]
        s0s = [s_ref[h] for h in heads]
        lhs = [_dot_nt(l1, s0.astype(BF16)) for l1, s0 in zip(l1s, s0s)]
        ps = [jnp.where(strict, -gm[0:q, 0:q], 0.0) for gm in gms]
        ts = [eye + p for p in ps]
        pbs = [p.astype(BF16) for p in ps]
        ps = [_dot(pb, pb) for pb in pbs]
        for lvl in range(1, nlev):
            pbs = [p.astype(BF16) for p in ps]
            if lvl < nlev - 1:
                boths = [_dot(jnp.concatenate([t, p], axis=0).astype(BF16), pb) for t, p, pb in zip(ts, ps, pbs)]
                ts = [t + both[0:q] for t, both in zip(ts, boths)]
                ps = [both[q:2 * q] for both in boths]
            else:
                ts = [t + _dot(t.astype(BF16), pb) for t, pb in zip(ts, pbs)]
        tops = [jnp.where(strict_k2, gm[0:q, :], 0.0).astype(BF16) for gm in gms]
        xms = [lh[0:q] + _dot(top, jnp.concatenate([zeros_v, vb], axis=0)) for lh, top, vb in zip(lhs, tops, vbs)]
        ums = [_dot(t.astype(BF16), xm.astype(BF16)).astype(BF16) for t, xm in zip(ts, xms)]
        uvs = [jnp.concatenate([um, vb], axis=0) for um, vb in zip(ums, vbs)]
        bots = [jnp.where(incl2, jnp.where(k_half, gm[q:2 * q, :], -gm[q:2 * q, :]), 0.0).astype(BF16) for gm in gms]
        for sl, lh, bot, uv in zip(sls, lhs, bots, uvs):
            y_ref[:, sl] = lh[q:2 * q] + _dot(bot, uv)
        for h, sl, s0, uv, r1e in zip(heads, sls, s0s, uvs, r1es):
            s_ref[h] = s0 * w_end[:, sl] + _dot_tn(uv, r1e)

    if final:
        a_b = 1.0 / (1.0 + jnp.exp(-(a0b_ref[...] + _dot_f32ish(
            u[:, 4 * R_DIM + 3 * R_LORA:4 * R_DIM + 4 * R_LORA], aupb_ref[...]))))
        k_sum = k_dir + k_all * (1.0 + (a_b - 1.0) * ka_ref[...])
        rkk = r_all * k_sum * rk_ref[...]
        g = _silu(u[:, 3 * R_DIM:4 * R_DIM])
        ytot = y_ref[...] + yb_ref[0]
        for h in range(R_HEADS):
            sl = slice(h * n, (h + 1) * n)
            yh = ytot[:, sl]
            mu_y = jnp.mean(yh, axis=-1, keepdims=True)
            dy = yh - mu_y
            var_y = jnp.mean(dy * dy, axis=-1, keepdims=True)
            bonus = jnp.sum(rkk[:, sl], axis=-1, keepdims=True) * v_all[:, sl]
            o_ref[0, :, sl] = (dy * lax.rsqrt(var_y + R_GN_EPS) * lnw_ref[:, sl] + lnb_ref[:, sl] + bonus) * g[:, sl]


def _rwkv_pass(rev, final, u, params, extra=()):
    bsz, seqlen, width = u.shape
    nc = seqlen // R_CHUNK
    r = R_CHUNK // SUBLANES
    nrow = seqlen // SUBLANES
    ci = (lambda c: nc - 1 - c) if rev else (lambda c: c)
    in_specs = [
        pl.BlockSpec((1, SUBLANES, width), lambda b, c: (b, jnp.maximum(ci(c) * r - 1, 0), 0)),
        pl.BlockSpec((1, R_CHUNK, width), lambda b, c: (b, ci(c), 0)),
        pl.BlockSpec((1, SUBLANES, width), lambda b, c: (b, jnp.minimum((ci(c) + 1) * r, nrow - 1), 0)),
    ]
    for p in params:
        in_specs.append(pl.BlockSpec(p.shape, lambda b, c: (0, 0)))
    scratch = [pltpu.VMEM((R_HEADS, R_HEADSIZE, R_HEADSIZE), F32),
               pltpu.VMEM((R_CHUNK + 2 * SUBLANES, width), F32)]
    if final:
        for p in extra[:-1]:
            in_specs.append(pl.BlockSpec(p.shape, lambda b, c: (0, 0)))
        in_specs.append(pl.BlockSpec((1, R_CHUNK, R_DIM), lambda b, c: (b, ci(c), 0)))
        scratch.append(pltpu.VMEM((R_CHUNK, R_DIM), F32))
    return pl.pallas_call(
        functools.partial(_rwkv_kernel, rev, final, nc),
        grid=(bsz, nc),
        in_specs=in_specs,
        out_specs=pl.BlockSpec((1, R_CHUNK, R_DIM), lambda b, c: (b, ci(c), 0)),
        out_shape=jax.ShapeDtypeStruct((bsz, seqlen, R_DIM), F32),
        scratch_shapes=scratch,
        compiler_params=_cparams(("parallel", "arbitrary")),
        name="rwkv_final" if final else "rwkv_bwd",
    )(u, u, u, *params, *extra)


def _rwkv_layer(x, bsz, seqlen, w_in, mu, w0, w_up, a0, a_up, k_k, k_a, r_k, ln_w, ln_b, w_out, ln_g, ln_bb, alpha):
    t = bsz * seqlen
    u = _proj(x, w_in.astype(BF16), tn=R_IN // 2).reshape(bsz, seqlen, R_IN)
    row = lambda p: p.reshape(1, -1)
    common = lambda d: (row(mu), row(w0[d]), w_up[d], row(a0[d]), a_up[d], row(k_k), row(k_a))
    y_b = _rwkv_pass(True, False, u, common(1))
    out = _rwkv_pass(False, True, u, common(0),
                     extra=(row(a0[1]), a_up[1], row(r_k), row(ln_w), row(ln_b), y_b))
    return _out_ln(out.reshape(t, R_DIM), w_out.astype(BF16), x, ln_g, ln_bb, alpha)


def _rope_tables(seqlen):
    rows = seqlen // GRID_W
    rowp = jnp.repeat(jnp.arange(rows, dtype=F32), GRID_W)
    colp = jnp.tile(jnp.arange(GRID_W, dtype=F32), rows)
    inv = ROPE_THETA ** (-jnp.arange(0, ROPE_AXIS_DIM, 2, dtype=F32) / ROPE_AXIS_DIM)
    ar = rowp[:, None] * inv[None]
    ac = colp[:, None] * inv[None]
    cos = jnp.concatenate([jnp.cos(ar), jnp.cos(ar), jnp.cos(ac), jnp.cos(ac)], axis=-1)
    sin = jnp.concatenate([-jnp.sin(ar), jnp.sin(ar), -jnp.sin(ac), jnp.sin(ac)], axis=-1)
    return cos, sin


def _qk_prep_kernel(scale, q_ref, kv_ref, cos_ref, sin_ref, qn_ref, kn_ref, qo_ref, ko_ref, vo_ref):
    cos = cos_ref[...]
    sin = sin_ref[...]
    lane = lax.broadcasted_iota(jnp.int32, cos.shape, 1)
    half = ROPE_AXIS_DIM // 2
    low = (lane & half) == 0

    def norm_rope(xh, gain, mult):
        ms = jnp.mean(xh * xh, axis=-1, keepdims=True)
        xn = xh * lax.rsqrt(ms + QK_NORM_EPS) * gain
        partner = jnp.where(low, pltpu.roll(xn, A_HEADDIM - half, 1), pltpu.roll(xn, half, 1))
        return (xn * cos + partner * sin) * mult

    for h in range(A_HEADS):
        sl = slice(h * A_HEADDIM, (h + 1) * A_HEADDIM)
        qo_ref[0, :, sl] = norm_rope(q_ref[0, :, sl], qn_ref[...], scale).astype(BF16)
    for h in range(A_KV_HEADS):
        sl = slice(h * A_HEADDIM, (h + 1) * A_HEADDIM)
        ko_ref[0, :, sl] = norm_rope(kv_ref[0, :, sl], kn_ref[...], 1.0).astype(BF16)
    vo_ref[0] = kv_ref[0, :, A_KV_DIM:2 * A_KV_DIM].astype(BF16)


def _attn_kernel(q_ref, k_ref, v_ref, g_ref, o_ref):
    rows = q_ref.shape[1] // A_SPLIT
    k = k_ref[0]
    v = v_ref[0]
    blocks = [slice(i * rows, (i + 1) * rows) for i in range(A_SPLIT)]
    s_next = _dot_nt(q_ref[0, blocks[0], :], k)
    for i, b in enumerate(blocks):
        s = s_next
        if i + 1 < A_SPLIT:
            s_next = _dot_nt(q_ref[0, blocks[i + 1], :], k)
        m = jnp.max(s, axis=-1, keepdims=True)
        p = jnp.exp(s - m)
        l = jnp.sum(p, axis=-1, keepdims=True)
        o_ref[0, b, :] = (_dot(p.astype(BF16), v) / l) * _silu(g_ref[0, b, :])


def _attn_layer(x, bsz, seqlen, w_in, q_norm, k_norm, w_out, ln_g, ln_b, alpha):
    t = bsz * seqlen
    wq = w_in[:, :A_DIM].astype(BF16)
    wkv = w_in[:, A_DIM:A_DIM + 2 * A_KV_DIM].astype(BF16)
    wg = w_in[:, A_DIM + 2 * A_KV_DIM:].astype(BF16)
    q = _proj(x, wq).reshape(bsz, seqlen, A_DIM)
    kv = _proj(x, wkv).reshape(bsz, seqlen, 2 * A_KV_DIM)
    gate = _proj(x, wg).reshape(bsz, seqlen, A_DIM)
    cos, sin = _rope_tables(seqlen)
    tl = 512
    tok = lambda width: pl.BlockSpec((1, tl, width), lambda b, i: (b, i, 0))
    tab = pl.BlockSpec((tl, A_HEADDIM), lambda b, i: (i, 0))
    par = pl.BlockSpec((1, A_HEADDIM), lambda b, i: (0, 0))
    qr, kr, vb = pl.pallas_call(
        functools.partial(_qk_prep_kernel, A_HEADDIM ** -0.5),
        grid=(bsz, seqlen // tl),
        in_specs=[tok(A_DIM), tok(2 * A_KV_DIM), tab, tab, par, par],
        out_specs=[tok(A_DIM), tok(A_KV_DIM), tok(A_KV_DIM)],
        out_shape=[jax.ShapeDtypeStruct((bsz, seqlen, A_DIM), BF16),
                   jax.ShapeDtypeStruct((bsz, seqlen, A_KV_DIM), BF16),
                   jax.ShapeDtypeStruct((bsz, seqlen, A_KV_DIM), BF16)],
        compiler_params=_cparams(("parallel", "parallel")),
        name="qk_prep",
    )(q, kv, cos, sin, q_norm.reshape(1, A_HEADDIM), k_norm.reshape(1, A_HEADDIM))
    tq = A_TQ
    o = pl.pallas_call(
        _attn_kernel,
        grid=(bsz, A_HEADS, seqlen // tq),
        in_specs=[pl.BlockSpec((1, tq, A_HEADDIM), lambda b, h, i: (b, i, h)),
                  pl.BlockSpec((1, seqlen, A_HEADDIM), lambda b, h, i: (b, 0, h // A_GROUP)),
                  pl.BlockSpec((1, seqlen, A_HEADDIM), lambda b, h, i: (b, 0, h // A_GROUP)),
                  pl.BlockSpec((1, tq, A_HEADDIM), lambda b, h, i: (b, i, h))],
        out_specs=pl.BlockSpec((1, tq, A_HEADDIM), lambda b, h, i: (b, i, h)),
        out_shape=jax.ShapeDtypeStruct((bsz, seqlen, A_DIM), F32),
        compiler_params=_cparams(("parallel", "parallel", "arbitrary")),
        name="attn",
    )(qr, kr, vb, gate)
    return _out_ln(o.reshape(t, A_DIM), w_out.astype(BF16), x, ln_g, ln_b, alpha)


def _trunk(x3, ln_g, ln_b, m_w_in, m_conv_w, m_conv_b, m_a_log, m_dt_bias, m_d, m_norm_w, m_w_out,
           r_w_in, r_mu, r_w0, r_w_up, r_a0, r_a_up, r_k_k, r_k_a, r_r_k, r_ln_w, r_ln_b, r_w_out,
           a_w_in, a_q_norm, a_k_norm, a_w_out):
    bsz, seqlen, _ = x3.shape
    alpha = (2.0 * DEPTH) ** 0.25
    x = x3.reshape(bsz * seqlen, D_MODEL)
    for i in range(DEPTH):
        j = i // N_MIXERS
        kind = i % N_MIXERS
        if kind == 0:
            x = _mamba_layer(x, bsz, seqlen, m_w_in[j], m_conv_w[j], m_conv_b[j], m_a_log[j], m_dt_bias[j],
                             m_d[j], m_norm_w[j], m_w_out[j], ln_g[i], ln_b[i], alpha)
        elif kind == 1:
            x = _rwkv_layer(x, bsz, seqlen, r_w_in[j], r_mu[j], r_w0[j], r_w_up[j], r_a0[j], r_a_up[j],
                            r_k_k[j], r_k_a[j], r_r_k[j].reshape(-1), r_ln_w[j], r_ln_b[j], r_w_out[j],
                            ln_g[i], ln_b[i], alpha)
        else:
            x = _attn_layer(x, bsz, seqlen, a_w_in[j], a_q_norm[j], a_k_norm[j], a_w_out[j],
                            ln_g[i], ln_b[i], alpha)
    return x.reshape(bsz, seqlen, D_MODEL)


def kernel(x_prompt, x_sample, ln_g, ln_b, m_w_in, m_conv_w, m_conv_b, m_a_log, m_dt_bias, m_d, m_norm_w, m_w_out, r_w_in, r_mu, r_w0, r_w_up, r_a0, r_a_up, r_k_k, r_k_a, r_r_k, r_ln_w, r_ln_b, r_w_out, a_w_in, a_q_norm, a_k_norm, a_w_out):
    weights = (ln_g, ln_b, m_w_in, m_conv_w, m_conv_b, m_a_log, m_dt_bias, m_d, m_norm_w, m_w_out,
               r_w_in, r_mu, r_w0, r_w_up, r_a0, r_a_up, r_k_k, r_k_a, r_r_k, r_ln_w, r_ln_b, r_w_out,
               a_w_in, a_q_norm, a_k_norm, a_w_out)
    return (_trunk(x_prompt, *weights), _trunk(x_sample, *weights))
```
